```python
import math
import jax
import jax.numpy as jnp
from jax import lax
import numpy as np

D_MODEL = 1024
BATCH = 4
SEQ = 8192
DEPTH = 2

PLE_DIM = 256
SB_HEADS = 8
SB_HEAD_DIM = 64
SB_WIDTH = SB_HEADS * SB_HEAD_DIM
SB_BLOCK = 128
SSD_HEADS = 8
SSD_HEAD_DIM = 64
SSD_WIDTH = SSD_HEADS * SSD_HEAD_DIM
SSD_STATE = 128
SSD_GROUPS = 2
SSD_CONV = 4
SSD_CHUNK = 128
XBC_WIDTH = SSD_WIDTH + 2 * SSD_GROUPS * SSD_STATE
IN_WIDTH = 3 * SB_WIDTH + SSD_WIDTH + XBC_WIDTH + SSD_HEADS
MIX_WIDTH = SB_WIDTH + SSD_WIDTH
CONF_WIDTH = D_MODEL
CONF_KERNEL = 31
D_FF = 2816
N_EXPERTS = 8
TOP_K = 2
D_FF_EXPERT = 3584
N_EVEN = (DEPTH + 1) // 2
N_ODD = DEPTH // 2
EPS = 1e-6

kernel_name = "hybrid_stickbreak_ssd_conformer_moe"


def rms_norm(x, g):
    xf = x.astype(jnp.float32)
    y = xf * lax.rsqrt(jnp.mean(xf * xf, axis=-1, keepdims=True) + EPS)
    return (y * g.astype(jnp.float32)).astype(x.dtype)


def layer_norm(x, g, b):
    xf = x.astype(jnp.float32)
    mu = jnp.mean(xf, axis=-1, keepdims=True)
    var = jnp.mean(jnp.square(xf - mu), axis=-1, keepdims=True)
    y = (xf - mu) * lax.rsqrt(var + EPS)
    return (y * g.astype(jnp.float32) + b.astype(jnp.float32)).astype(x.dtype)


def causal_depthwise_conv(x, w, b):
    k, c = w.shape
    y = lax.conv_general_dilated(
        x, w[:, None, :].astype(x.dtype), window_strides=(1,),
        padding=[(k - 1, 0)], dimension_numbers=("NWC", "WIO", "NWC"),
        feature_group_count=c)
    return y + b.astype(x.dtype)


def stick_breaking_attention(q, k, v):
    b, s, h, d = q.shape
    nb = s // SB_BLOCK
    qb = q.reshape(b, nb, SB_BLOCK, h, d).transpose(1, 0, 3, 2, 4)
    kt = k.transpose(0, 2, 1, 3)
    vt = v.transpose(0, 2, 1, 3)
    key_pos = jnp.arange(s)
    scale = 1.0 / math.sqrt(d)

    def block(args):
        qi, i = args
        z = jnp.einsum("bhqd,bhkd->bhqk", qi, kt).astype(jnp.float32) * scale
        q_pos = i * SB_BLOCK + jnp.arange(SB_BLOCK)
        mask = key_pos[None, :] < q_pos[:, None]
        log_beta = jax.nn.log_sigmoid(z)
        log_1m_beta = jnp.where(mask, jax.nn.log_sigmoid(-z), 0.0)
        tail = lax.cumsum(log_1m_beta, axis=3, reverse=True) - log_1m_beta
        w = jnp.where(mask, jnp.exp(log_beta + tail), 0.0)
        return jnp.einsum("bhqk,bhkd->bhqd", w.astype(vt.dtype), vt)

    out = lax.map(block, (qb, jnp.arange(nb)))
    return out.transpose(1, 0, 3, 2, 4).reshape(b, s, h, d)


def ssd_chunked(x, dt, a, bm, cm, d_skip):
    b, s, h, pdim = x.shape
    n = bm.shape[-1]
    nc = s // SSD_CHUNK
    f32 = jnp.float32
    xc = x.reshape(b, nc, SSD_CHUNK, h, pdim).astype(f32)
    bc = bm.reshape(b, nc, SSD_CHUNK, h, n).astype(f32)
    cc = cm.reshape(b, nc, SSD_CHUNK, h, n).astype(f32)
    dtc = dt.reshape(b, nc, SSD_CHUNK, h)
    xdt = xc * dtc[..., None]
    a_cum = jnp.cumsum(dtc * a, axis=2).transpose(0, 1, 3, 2)
    seg = a_cum[..., :, None] - a_cum[..., None, :]
    causal = jnp.tril(jnp.ones((SSD_CHUNK, SSD_CHUNK), dtype=bool))
    lmat = jnp.exp(jnp.where(causal, seg, -jnp.inf))
    g = jnp.einsum("bclhn,bcshn->bchls", cc, bc) * lmat
    y_diag = jnp.einsum("bchls,bcshp->bclhp", g, xdt)
    decay_to_end = jnp.exp(a_cum[..., -1:] - a_cum)
    states = jnp.einsum("bcshn,bchs,bcshp->bchpn", bc, decay_to_end, xdt)
    chunk_decay = jnp.exp(a_cum[..., -1])

    def step(carry, inp):
        st, dec = inp
        return carry * dec[..., None, None] + st, carry

    init = jnp.zeros((b, h, pdim, n), f32)
    _, h_prev = lax.scan(step, init, (states.transpose(1, 0, 2, 3, 4), chunk_decay.transpose(1, 0, 2)))
    h_prev = h_prev.transpose(1, 0, 2, 3, 4)
    y_off = jnp.einsum("bclhn,bchpn,bchl->bclhp", cc, h_prev, jnp.exp(a_cum))
    y = y_diag + y_off + xc * d_skip.astype(f32)[:, None]
    return y.reshape(b, s, h, pdim)


def even_mixer(h, w_in, conv_w, conv_b, dt_bias, a_log, d_skip, ssd_norm_g, w_out):
    b, s, _ = h.shape
    proj = h @ w_in
    cuts = [SB_WIDTH, 2 * SB_WIDTH, 3 * SB_WIDTH, 3 * SB_WIDTH + SSD_WIDTH,
            3 * SB_WIDTH + SSD_WIDTH + XBC_WIDTH]
    q, k, v, z, xbc, dt_raw = jnp.split(proj, cuts, axis=-1)
    q = q.reshape(b, s, SB_HEADS, SB_HEAD_DIM)
    k = k.reshape(b, s, SB_HEADS, SB_HEAD_DIM)
    v = v.reshape(b, s, SB_HEADS, SB_HEAD_DIM)
    attn = stick_breaking_attention(q, k, v).reshape(b, s, SB_WIDTH)
    xbc = jax.nn.silu(causal_depthwise_conv(xbc, conv_w, conv_b))
    xs, bm, cm = jnp.split(xbc, [SSD_WIDTH, SSD_WIDTH + SSD_GROUPS * SSD_STATE], axis=-1)
    rep = SSD_HEADS // SSD_GROUPS
    xs = xs.reshape(b, s, SSD_HEADS, SSD_HEAD_DIM)
    bm = jnp.repeat(bm.reshape(b, s, SSD_GROUPS, SSD_STATE), rep, axis=2)
    cm = jnp.repeat(cm.reshape(b, s, SSD_GROUPS, SSD_STATE), rep, axis=2)
    dt = jax.nn.softplus((dt_raw + dt_bias).astype(jnp.float32))
    a = -jnp.exp(a_log.astype(jnp.float32))
    y = ssd_chunked(xs, dt, a, bm, cm, d_skip).reshape(b, s, SSD_WIDTH)
    y = rms_norm(y * jax.nn.silu(z.astype(jnp.float32)), ssd_norm_g).astype(h.dtype)
    return jnp.concatenate([attn, y], axis=-1) @ w_out


def conformer_conv(h, pw1_w, pw1_b, dw_w, dw_b, ln_g, ln_b, pw2_w, pw2_b):
    u = h @ pw1_w + pw1_b
    val, gate = jnp.split(u, 2, axis=-1)
    u = val * jax.nn.sigmoid(gate)
    u = causal_depthwise_conv(u, dw_w, dw_b)
    u = jax.nn.silu(layer_norm(u, ln_g, ln_b))
    return u @ pw2_w + pw2_b


def swiglu(h, w1, w3, w2):
    return (jax.nn.silu(h @ w1) * (h @ w3)) @ w2


def moe_swiglu(h, router_w, w1, w3, w2):
    b, s, d = h.shape
    t = h.reshape(b * s, d)
    logits = (t @ router_w).astype(jnp.float32)
    top_val, top_idx = lax.top_k(logits, TOP_K)
    gates = jax.nn.softmax(top_val, axis=-1)
    combine = jnp.sum(jax.nn.one_hot(top_idx, N_EXPERTS, dtype=jnp.float32) * gates[..., None], axis=1)
    combine = combine.astype(t.dtype)
    out = jnp.zeros_like(t)
    for e in range(N_EXPERTS):
        out = out + combine[:, e:e + 1] * swiglu(t, w1[e], w3[e], w2[e])
    return out.reshape(b, s, d)


def setup_inputs(seed: int = 0) -> dict:
    key = jax.random.key(seed)
    ks = iter(jax.random.split(key, 32))
    f32 = jnp.float32

    def nrm(shape, fan_in):
        return jax.random.normal(next(ks), shape, f32) * fan_in ** -0.5

    def gain(shape):
        return 1.0 + 0.05 * jax.random.normal(next(ks), shape, f32)

    def small(shape):
        return 0.02 * jax.random.normal(next(ks), shape, f32)

    x = jax.random.normal(next(ks), (BATCH, SEQ, D_MODEL), f32)
    p = jax.random.normal(next(ks), (DEPTH, BATCH, SEQ, PLE_DIM), f32)
    mix_norm_g = gain((DEPTH, D_MODEL))
    ffn_norm_g = gain((DEPTH, D_MODEL))
    ple_norm_g = gain((DEPTH, D_MODEL))
    w_ple = nrm((DEPTH, PLE_DIM, D_MODEL), PLE_DIM)
    w_ple_gate = nrm((DEPTH, D_MODEL, D_MODEL), D_MODEL)

    ev_w_in = nrm((N_EVEN, D_MODEL, IN_WIDTH), D_MODEL)
    ev_conv_w = nrm((N_EVEN, SSD_CONV, XBC_WIDTH), SSD_CONV)
    ev_conv_b = small((N_EVEN, XBC_WIDTH))
    u = jax.random.uniform(next(ks), (N_EVEN, SSD_HEADS), f32)
    dt0 = jnp.exp(u * (math.log(0.1) - math.log(0.001)) + math.log(0.001))
    ev_dt_bias = dt0 + jnp.log(-jnp.expm1(-dt0))
    ev_a_log = jnp.log(jax.random.uniform(next(ks), (N_EVEN, SSD_HEADS), f32, minval=1.0, maxval=16.0))
    ev_d = gain((N_EVEN, SSD_HEADS))
    ev_ssd_norm_g = gain((N_EVEN, SSD_WIDTH))
    ev_w_out = nrm((N_EVEN, MIX_WIDTH, D_MODEL), MIX_WIDTH)
    ev_ffn_w1 = nrm((N_EVEN, D_MODEL, D_FF), D_MODEL)
    ev_ffn_w3 = nrm((N_EVEN, D_MODEL, D_FF), D_MODEL)
    ev_ffn_w2 = nrm((N_EVEN, D_FF, D_MODEL), D_FF)

    od_pw1_w = nrm((N_ODD, D_MODEL, 2 * CONF_WIDTH), D_MODEL)
    od_pw1_b = small((N_ODD, 2 * CONF_WIDTH))
    od_dw_w = nrm((N_ODD, CONF_KERNEL, CONF_WIDTH), CONF_KERNEL)
    od_dw_b = small((N_ODD, CONF_WIDTH))
    od_ln_g = gain((N_ODD, CONF_WIDTH))
    od_ln_b = small((N_ODD, CONF_WIDTH))
    od_pw2_w = nrm((N_ODD, CONF_WIDTH, D_MODEL), CONF_WIDTH)
    od_pw2_b = small((N_ODD, D_MODEL))
    od_router_w = nrm((N_ODD, D_MODEL, N_EXPERTS), D_MODEL)
    od_moe_w1 = nrm((N_ODD, N_EXPERTS, D_MODEL, D_FF_EXPERT), D_MODEL)
    od_moe_w3 = nrm((N_ODD, N_EXPERTS, D_MODEL, D_FF_EXPERT), D_MODEL)
    od_moe_w2 = nrm((N_ODD, N_EXPERTS, D_FF_EXPERT, D_MODEL), D_FF_EXPERT)
    final_norm_g = gain((D_MODEL,))
    return {
        "x": x, "p": p,
        "mix_norm_g": mix_norm_g, "ffn_norm_g": ffn_norm_g, "ple_norm_g": ple_norm_g,
        "w_ple": w_ple, "w_ple_gate": w_ple_gate,
        "ev_w_in": ev_w_in, "ev_conv_w": ev_conv_w, "ev_conv_b": ev_conv_b,
        "ev_dt_bias": ev_dt_bias, "ev_a_log": ev_a_log, "ev_d": ev_d,
        "ev_ssd_norm_g": ev_ssd_norm_g, "ev_w_out": ev_w_out,
        "ev_ffn_w1": ev_ffn_w1, "ev_ffn_w3": ev_ffn_w3, "ev_ffn_w2": ev_ffn_w2,
        "od_pw1_w": od_pw1_w, "od_pw1_b": od_pw1_b, "od_dw_w": od_dw_w, "od_dw_b": od_dw_b,
        "od_ln_g": od_ln_g, "od_ln_b": od_ln_b, "od_pw2_w": od_pw2_w, "od_pw2_b": od_pw2_b,
        "od_router_w": od_router_w, "od_moe_w1": od_moe_w1, "od_moe_w3": od_moe_w3,
        "od_moe_w2": od_moe_w2,
        "final_norm_g": final_norm_g,
    }


def reference(x, p, mix_norm_g, ffn_norm_g, ple_norm_g, w_ple, w_ple_gate,
              ev_w_in, ev_conv_w, ev_conv_b, ev_dt_bias, ev_a_log, ev_d,
              ev_ssd_norm_g, ev_w_out, ev_ffn_w1, ev_ffn_w3, ev_ffn_w2,
              od_pw1_w, od_pw1_b, od_dw_w, od_dw_b, od_ln_g, od_ln_b,
              od_pw2_w, od_pw2_b, od_router_w, od_moe_w1, od_moe_w3, od_moe_w2,
              final_norm_g):
    h = x
    for i in range(DEPTH):
        j = i // 2
        hn = rms_norm(h, mix_norm_g[i])
        if i % 2 == 0:
            h = h + even_mixer(hn, ev_w_in[j], ev_conv_w[j], ev_conv_b[j], ev_dt_bias[j],
                               ev_a_log[j], ev_d[j], ev_ssd_norm_g[j], ev_w_out[j])
            h = h + swiglu(rms_norm(h, ffn_norm_g[i]), ev_ffn_w1[j], ev_ffn_w3[j], ev_ffn_w2[j])
        else:
            h = h + conformer_conv(hn, od_pw1_w[j], od_pw1_b[j], od_dw_w[j], od_dw_b[j],
                                   od_ln_g[j], od_ln_b[j], od_pw2_w[j], od_pw2_b[j])
            h = h + moe_swiglu(rms_norm(h, ffn_norm_g[i]), od_router_w[j],
                               od_moe_w1[j], od_moe_w3[j], od_moe_w2[j])
        gate = jax.nn.sigmoid(rms_norm(h, ple_norm_g[i]) @ w_ple_gate[i])
        h = h + (p[i] @ w_ple[i]) * gate
    return rms_norm(h, final_norm_g)
```

```python
import functools
import math

import jax
import jax.numpy as jnp
from jax import lax
from jax.experimental import pallas as pl
from jax.experimental.pallas import tpu as pltpu

F32 = jnp.float32
BF16 = jnp.bfloat16
EPS = 1e-6
LOG2E = 1.4426950408889634

V7X_LANES = 128
V7X_SUBLANES = 8
V7X_VMEM_BYTES = 64 * 1024 * 1024

SB_HEADS = 8
SB_HEAD_DIM = 64
SSD_HEADS = 8
SSD_HEAD_DIM = 64
SSD_STATE = 128
SSD_GROUPS = 2
SSD_CONV = 4
N_EXPERTS = 8
TOP_K = 2

ROW_TILE = 512
FFN_ROW_TILE = 1024
FFN_COL_TILE = 256
ATTN_BLOCK = 256
SSD_CHUNK = 128
CONV_ROW_TILE = 256
CONV_ROW_CHUNK = 32
CONV_HALO = 32


def _tile(n, pref, mult=V7X_SUBLANES):
    if n <= pref:
        return n
    t = (pref // mult) * mult
    while t > mult and n % t:
        t -= mult
    assert n % t == 0, (n, pref, mult)
    return t


def _vmem_limit(block_bytes, scratch_bytes=0, temp_bytes=0):
    need = 2 * block_bytes + scratch_bytes + temp_bytes + (4 << 20)
    return int(min(max(need, 16 << 20), V7X_VMEM_BYTES - (6 << 20)))


def _params(sem, vmem):
    return pltpu.CompilerParams(dimension_semantics=sem, vmem_limit_bytes=vmem)


def _nbytes(shape, dtype):
    return math.prod(shape) * jnp.dtype(dtype).itemsize


def _rms(x, g):
    return x * lax.rsqrt(jnp.mean(x * x, axis=-1, keepdims=True) + EPS) * g


def _sigmoid(x):
    return 1.0 / (1.0 + jnp.exp(-x))


def _silu(x):
    return x * _sigmoid(x)


def _softplus(x):
    return jnp.maximum(x, 0.0) + jnp.log(1.0 + jnp.exp(-jnp.abs(x)))


def _split3(x):
    hi = x.astype(BF16)
    r1 = x - hi.astype(F32)
    mid = r1.astype(BF16)
    lo = (r1 - mid.astype(F32)).astype(BF16)
    return hi, mid, lo


def _dot_exact_rhs(x, m):
    hi, mid, lo = _split3(x)
    d = functools.partial(jnp.dot, preferred_element_type=F32)
    return d(hi, m) + d(mid, m) + d(lo, m)


def _dot_exact_lhs(m, x):
    hi, mid, lo = _split3(x)
    d = functools.partial(jnp.dot, preferred_element_type=F32)
    return d(m, hi) + d(m, mid) + d(m, lo)


def _in_proj_kernel(x_ref, g_ref, w_ref, qkv_ref, z_ref, xbc_ref, dt_ref, *, sbw, zw, xbcw, qscale):
    hn = _rms(x_ref[...], g_ref[...]).astype(BF16)

    def mm(lo, hi):
        return jnp.dot(hn, w_ref[:, lo:hi], preferred_element_type=F32)

    qkv_ref[:, 0:sbw] = (mm(0, sbw) * qscale).astype(BF16)
    qkv_ref[:, sbw:3 * sbw] = mm(sbw, 3 * sbw).astype(BF16)
    o = 3 * sbw
    z_ref[...] = mm(o, o + zw)
    o += zw
    xbc_ref[...] = mm(o, o + xbcw)
    o += xbcw
    dt_ref[...] = mm(o, o + V7X_LANES)


def _in_proj(x2, g, w_pad, *, sbw, zw, xbcw, qscale):
    t, d = x2.shape
    n = w_pad.shape[1]
    tm = _tile(t, ROW_TILE)
    blocks = (_nbytes((tm, d), F32) + _nbytes((d, n), BF16) + _nbytes((tm, 3 * sbw), BF16)
              + _nbytes((tm, zw + xbcw + V7X_LANES), F32))
    return pl.pallas_call(
        functools.partial(_in_proj_kernel, sbw=sbw, zw=zw, xbcw=xbcw, qscale=qscale),
        grid=(t // tm,),
        in_specs=[pl.BlockSpec((tm, d), lambda i: (i, 0)),
                  pl.BlockSpec((1, d), lambda i: (0, 0)),
                  pl.BlockSpec((d, n), lambda i: (0, 0))],
        out_specs=[pl.BlockSpec((tm, 3 * sbw), lambda i: (i, 0)),
                   pl.BlockSpec((tm, zw), lambda i: (i, 0)),
                   pl.BlockSpec((tm, xbcw), lambda i: (i, 0)),
                   pl.BlockSpec((tm, V7X_LANES), lambda i: (i, 0))],
        out_shape=[jax.ShapeDtypeStruct((t, 3 * sbw), BF16),
                   jax.ShapeDtypeStruct((t, zw), F32),
                   jax.ShapeDtypeStruct((t, xbcw), F32),
                   jax.ShapeDtypeStruct((t, V7X_LANES), F32)],
        compiler_params=_params(("parallel",), _vmem_limit(blocks, 0, _nbytes((tm, n), F32))),
        name="in_proj",
    )(x2, g, w_pad)


def _sb_attn_kernel(q_ref, k_ref, v_ref, m_ref, o_ref, acc_ref, car_ref, *, blk, hd):
    qi = pl.program_id(2)
    lane = lax.broadcasted_iota(jnp.int32, (1, 2 * hd), 1)
    q = q_ref[...]
    qz = jnp.zeros_like(q)
    qh = (jnp.where(lane < hd, q, qz), jnp.where(lane >= hd, q, qz))
    acc_ref[...] = jnp.zeros_like(acc_ref)
    car_ref[...] = jnp.zeros_like(car_ref)
    rep = blk // V7X_LANES

    def do_block(j, dmask):
        start = pl.multiple_of(j * blk, blk)
        kblk = k_ref[pl.ds(start, blk), :]
        vblk = v_ref[pl.ds(start, blk), :]
        mm = m_ref[...]
        for h in range(2):
            z2 = lax.dot_general(qh[h], kblk, (((1,), (1,)), ((), ())), preferred_element_type=F32)
            sp2 = jnp.maximum(z2, 0.0) + jnp.log2(1.0 + jnp.exp2(-jnp.abs(z2)))
            if dmask is not None:
                sp2 = jnp.where(dmask, sp2, 0.0)
            ie = jnp.dot(sp2.astype(BF16), mm, preferred_element_type=F32)
            car = car_ref[h]
            w = jnp.exp2(z2 - ie[:, :blk] - jnp.concatenate([car] * rep, axis=1))
            if dmask is not None:
                w = jnp.where(dmask, w, 0.0)
            acc_ref[h] += jnp.dot(w.astype(BF16), vblk, preferred_element_type=F32)
            car_ref[h] = car + ie[:, blk:]

    row = lax.broadcasted_iota(jnp.int32, (blk, blk), 0)
    col = lax.broadcasted_iota(jnp.int32, (blk, blk), 1)
    do_block(qi, col < row)

    def body(i, c):
        do_block(qi - 1 - i, None)
        return c

    lax.fori_loop(0, qi, body, 0)
    o_ref[...] = jnp.where(lane < hd, acc_ref[0], acc_ref[1]).astype(o_ref.dtype)


def _sb_attention(qkv, b, s, *, heads, hd):
    t = qkv.shape[0]
    blk = _tile(s, ATTN_BLOCK, V7X_LANES)
    nq = s // blk
    hp = heads // 2
    width = 2 * hd
    assert width == V7X_LANES
    j = jnp.arange(blk)[:, None]
    c = jnp.arange(blk + V7X_LANES)[None, :]
    m_ext = ((j >= c) | (c >= blk)).astype(BF16)
    blocks = (_nbytes((blk, width), BF16) * 2 + 2 * _nbytes((s, width), BF16)
              + _nbytes(m_ext.shape, BF16))
    scratch = 4 * _nbytes((blk, width), F32)
    temps = 8 * _nbytes((blk, blk + V7X_LANES), F32)
    return pl.pallas_call(
        functools.partial(_sb_attn_kernel, blk=blk, hd=hd),
        grid=(b, hp, nq),
        in_specs=[pl.BlockSpec((blk, width), lambda bi, h, qi: (bi * nq + qi, h)),
                  pl.BlockSpec((s, width), lambda bi, h, qi: (bi, hp + h)),
                  pl.BlockSpec((s, width), lambda bi, h, qi: (bi, 2 * hp + h)),
                  pl.BlockSpec(m_ext.shape, lambda bi, h, qi: (0, 0))],
        out_specs=pl.BlockSpec((blk, width), lambda bi, h, qi: (bi * nq + qi, h)),
        out_shape=jax.ShapeDtypeStruct((t, heads * hd), BF16),
        scratch_shapes=[pltpu.VMEM((2, blk, width), F32), pltpu.VMEM((2, blk, width), F32)],
        compiler_params=_params(("parallel", "parallel", "arbitrary"), _vmem_limit(blocks, scratch, temps)),
        name="sb_attention",
    )(qkv, qkv, qkv, m_ext)


def _ssd_kernel(xbc_ref, z_ref, dt_ref, cw_ref, cb_ref, dtb_ref, alog_ref, dskip_ref, ng_ref, tri_ref, exp_ref,
                o_ref, ext_ref, st_ref, *, q, width, nstate, hd, kconv):
    halo = V7X_SUBLANES
    gw = width // SSD_GROUPS

    @pl.when(pl.program_id(1) == 0)
    def _():
        ext_ref[0:halo, :] = jnp.zeros((halo, ext_ref.shape[1]), F32)
        st_ref[...] = jnp.zeros_like(st_ref)

    x_new = xbc_ref[...]
    ext_ref[halo:halo + q, :] = x_new
    conv = cb_ref[...] + cw_ref[0:1, :] * ext_ref[pl.ds(halo - (kconv - 1), q), :]
    for k in range(1, kconv):
        conv = conv + cw_ref[k:k + 1, :] * ext_ref[pl.ds(halo - (kconv - 1) + k, q), :]
    ext_ref[0:halo, :] = x_new[q - halo:q, :]
    xbc = _silu(conv)
    xs = xbc[:, :width]
    bm = xbc[:, width:width + SSD_GROUPS * nstate]
    cm = xbc[:, width + SSD_GROUPS * nstate:]

    dt = _softplus(dt_ref[...] + dtb_ref[...])
    a = -jnp.exp(alog_ref[...])
    acum = _dot_exact_lhs(tri_ref[...], dt * a)
    ea = jnp.exp(acum)
    dte = jnp.exp(acum[q - 1:q, :] - acum)
    ex = exp_ref[...]
    dt_x = _dot_exact_rhs(dt, ex)
    ea_x = _dot_exact_rhs(ea, ex)
    dte_x = _dot_exact_rhs(dte, ex)
    xdt = xs * dt_x
    xdt_b = xdt.astype(BF16)
    xdec_b = (xdt * dte_x).astype(BF16)
    acum_t = acum.T

    row = lax.broadcasted_iota(jnp.int32, (q, q), 0)
    col = lax.broadcasted_iota(jnp.int32, (q, q), 1)
    causal = col <= row
    lane = lax.broadcasted_iota(jnp.int32, (1, 2 * hd), 1)
    heads_per_group = gw // hd
    y_parts = []
    for g in range(SSD_GROUPS):
        bg = bm[:, g * nstate:(g + 1) * nstate].astype(BF16)
        cg = cm[:, g * nstate:(g + 1) * nstate].astype(BF16)
        gmat = lax.dot_general(cg, bg, (((1,), (1,)), ((), ())), preferred_element_type=F32)
        for pr in range(heads_per_group // 2):
            c0 = g * gw + pr * 2 * hd
            xpair = xdt_b[:, c0:c0 + 2 * hd]
            yd = []
            for hh in range(2):
                h = (c0 // hd) + hh
                seg = acum[:, h:h + 1] - acum_t[h:h + 1, :]
                lmat = jnp.exp(jnp.where(causal, seg, -jnp.inf))
                yd.append(jnp.dot((gmat * lmat).astype(BF16), xpair, preferred_element_type=F32))
            y_parts.append(jnp.where(lane < hd, yd[0], yd[1]))
        hprev = st_ref[:, g * gw:(g + 1) * gw]
        y_off = jnp.dot(cg, hprev.astype(BF16), preferred_element_type=F32) * ea_x[:, g * gw:(g + 1) * gw]
        st_new = lax.dot_general(bg, xdec_b[:, g * gw:(g + 1) * gw], (((0,), (0,)), ((), ())),
                                 preferred_element_type=F32)
        st_ref[:, g * gw:(g + 1) * gw] = hprev * ea_x[q - 1:q, g * gw:(g + 1) * gw] + st_new
        y_parts.append(y_off)
    pp = heads_per_group // 2 + 1
    ys = []
    for g in range(SSD_GROUPS):
        diag = jnp.concatenate(y_parts[g * pp:g * pp + pp - 1], axis=1)
        ys.append(diag + y_parts[g * pp + pp - 1])
    y = jnp.concatenate(ys, axis=1) + xs * dskip_ref[...]
    o_ref[...] = _rms(y * _silu(z_ref[...]), ng_ref[...]).astype(o_ref.dtype)


def _ssd(xbc, z, dt, conv_w, conv_b, dt_bias, a_log, d_skip, norm_g, b, s, *, heads, hd, nstate):
    t, xbcw = xbc.shape
    width = heads * hd
    q = _tile(s, SSD_CHUNK, V7X_LANES)
    nc = s // q
    kconv = conv_w.shape[0]
    cw = jnp.zeros((V7X_SUBLANES, xbcw), F32).at[:kconv].set(conv_w)
    pad = lambda v: jnp.zeros((1, V7X_LANES), F32).at[0, :heads].set(v)
    tri = (jnp.arange(q)[None, :] <= jnp.arange(q)[:, None]).astype(BF16)
    hx = (jnp.arange(V7X_LANES)[:, None] == (jnp.arange(width)[None, :] // hd)).astype(BF16)
    row = lambda i, c: (i * nc + c, 0)
    fix = lambda i, c: (0, 0)
    blocks = (_nbytes((q, xbcw + width + V7X_LANES), F32) + _nbytes((q, width), BF16)
              + _nbytes((q, q), BF16) + _nbytes((V7X_LANES, width), BF16) + 8 * _nbytes((1, xbcw), F32))
    scratch = _nbytes((q + V7X_SUBLANES, xbcw), F32) + _nbytes((nstate, width), F32)
    temps = 24 * _nbytes((q, xbcw), F32)
    return pl.pallas_call(
        functools.partial(_ssd_kernel, q=q, width=width, nstate=nstate, hd=hd, kconv=kconv),
        grid=(b, nc),
        in_specs=[pl.BlockSpec((q, xbcw), row), pl.BlockSpec((q, width), row), pl.BlockSpec((q, V7X_LANES), row),
                  pl.BlockSpec((V7X_SUBLANES, xbcw), fix), pl.BlockSpec((1, xbcw), fix),
                  pl.BlockSpec((1, V7X_LANES), fix), pl.BlockSpec((1, V7X_LANES), fix),
                  pl.BlockSpec((1, width), fix), pl.BlockSpec((1, width), fix),
                  pl.BlockSpec((q, q), fix), pl.BlockSpec((V7X_LANES, width), fix)],
        out_specs=pl.BlockSpec((q, width), row),
        out_shape=jax.ShapeDtypeStruct((t, width), BF16),
        scratch_shapes=[pltpu.VMEM((q + V7X_SUBLANES, xbcw), F32), pltpu.VMEM((nstate, width), F32)],
        compiler_params=_params(("parallel", "arbitrary"), _vmem_limit(blocks, scratch, temps)),
        name="ssd",
    )(xbc, z, dt, cw, conv_b.reshape(1, xbcw), pad(dt_bias), pad(a_log),
      jnp.repeat(d_skip, hd).reshape(1, width), norm_g.reshape(1, width), tri, hx)


def _out_proj_kernel(x_ref, a_ref, y_ref, w_ref, o_ref, *, wa):
    acc = jnp.dot(a_ref[...], w_ref[0:wa, :], preferred_element_type=F32)
    acc = acc + jnp.dot(y_ref[...], w_ref[wa:, :], preferred_element_type=F32)
    o_ref[...] = x_ref[...] + acc


def _out_proj(x2, attn, ynorm, w):
    t, d = x2.shape
    wa, wy = attn.shape[1], ynorm.shape[1]
    tm = _tile(t, ROW_TILE)
    row = lambda i: (i, 0)
    blocks = 2 * _nbytes((tm, d), F32) + _nbytes((tm, wa + wy), BF16) + _nbytes(w.shape, BF16)
    return pl.pallas_call(
        functools.partial(_out_proj_kernel, wa=wa),
        grid=(t // tm,),
        in_specs=[pl.BlockSpec((tm, d), row), pl.BlockSpec((tm, wa), row), pl.BlockSpec((tm, wy), row),
                  pl.BlockSpec(w.shape, lambda i: (0, 0))],
        out_specs=pl.BlockSpec((tm, d), row),
        out_shape=jax.ShapeDtypeStruct((t, d), F32),
        compiler_params=_params(("parallel",), _vmem_limit(blocks, 0, 2 * _nbytes((tm, d), F32))),
        name="out_proj",
    )(x2, attn, ynorm, w)


def _ffn_kernel(h_ref, g_ref, w1_ref, w3_ref, w2_ref, o_ref, n_ref, acc_ref):
    f = pl.program_id(1)

    @pl.when(f == 0)
    def _():
        n_ref[...] = _rms(h_ref[...], g_ref[...]).astype(BF16)
        acc_ref[...] = jnp.zeros_like(acc_ref)

    n = n_ref[...]
    a = _silu(jnp.dot(n, w1_ref[...], preferred_element_type=F32)) * jnp.dot(n, w3_ref[...], preferred_element_type=F32)
    acc_ref[...] += jnp.dot(a.astype(BF16), w2_ref[...], preferred_element_type=F32)

    @pl.when(f == pl.num_programs(1) - 1)
    def _():
        o_ref[...] = h_ref[...] + acc_ref[...]


def _ffn(h, g, w1, w3, w2):
    t, d = h.shape
    ff = w1.shape[1]
    tm = _tile(t, FFN_ROW_TILE)
    tf = _tile(ff, FFN_COL_TILE, V7X_LANES)
    blocks = 2 * _nbytes((tm, d), F32) + 3 * _nbytes((d, tf), BF16)
    scratch = _nbytes((tm, d), BF16) + _nbytes((tm, d), F32)
    temps = 4 * _nbytes((tm, tf), F32) + _nbytes((tm, d), F32)
    return pl.pallas_call(
        _ffn_kernel,
        grid=(t // tm, ff // tf),
        in_specs=[pl.BlockSpec((tm, d), lambda i, f: (i, 0)), pl.BlockSpec((1, d), lambda i, f: (0, 0)),
                  pl.BlockSpec((d, tf), lambda i, f: (0, f)), pl.BlockSpec((d, tf), lambda i, f: (0, f)),
                  pl.BlockSpec((tf, d), lambda i, f: (f, 0))],
        out_specs=pl.BlockSpec((tm, d), lambda i, f: (i, 0)),
        out_shape=jax.ShapeDtypeStruct((t, d), F32),
        scratch_shapes=[pltpu.VMEM((tm, d), BF16), pltpu.VMEM((tm, d), F32)],
        compiler_params=_params(("parallel", "arbitrary"), _vmem_limit(blocks, scratch, temps)),
        name="ffn",
    )(h, g, w1, w3, w2)


def _ple_kernel(h_ref, p_ref, g_ref, wg_ref, wp_ref, fg_ref, o_ref, *, final_norm):
    h = h_ref[...]
    n = _rms(h, g_ref[...]).astype(BF16)
    gate = _sigmoid(jnp.dot(n, wg_ref[...], preferred_element_type=F32))
    emb = jnp.dot(p_ref[...].astype(BF16), wp_ref[...], preferred_element_type=F32)
    out = h + emb * gate
    if final_norm:
        out = _rms(out, fg_ref[...])
    o_ref[...] = out


def _ple(h, p2, g, w_gate, w_ple, final_g=None):
    t, d = h.shape
    pd = p2.shape[1]
    tm = _tile(t, ROW_TILE)
    row = lambda i: (i, 0)
    fix = lambda i: (0, 0)
    fg = jnp.ones((1, d), F32) if final_g is None else final_g
    blocks = 2 * _nbytes((tm, d), F32) + _nbytes((tm, pd), F32) + _nbytes((d + pd, d), BF16)
    return pl.pallas_call(
        functools.partial(_ple_kernel, final_norm=final_g is not None),
        grid=(t // tm,),
        in_specs=[pl.BlockSpec((tm, d), row), pl.BlockSpec((tm, pd), row), pl.BlockSpec((1, d), fix),
                  pl.BlockSpec((d, d), fix), pl.BlockSpec((pd, d), fix), pl.BlockSpec((1, d), fix)],
        out_specs=pl.BlockSpec((tm, d), row),
        out_shape=jax.ShapeDtypeStruct((t, d), F32),
        compiler_params=_params(("parallel",), _vmem_limit(blocks, 0, 4 * _nbytes((tm, d), F32))),
        name="ple_final" if final_g is not None else "ple",
    )(h, p2, g, w_gate, w_ple, fg)


def _glu_kernel(h_ref, g_ref, w_ref, b_ref, o_ref, *, c):
    n = _rms(h_ref[...], g_ref[...]).astype(BF16)
    val = jnp.dot(n, w_ref[:, 0:c], preferred_element_type=F32) + b_ref[:, 0:c]
    gate = jnp.dot(n, w_ref[:, c:], preferred_element_type=F32) + b_ref[:, c:]
    o_ref[...] = val * _sigmoid(gate)


def _glu(h, g, w, bias):
    t, d = h.shape
    c = w.shape[1] // 2
    tm = _tile(t, ROW_TILE)
    row = lambda i: (i, 0)
    fix = lambda i: (0, 0)
    blocks = _nbytes((tm, d), F32) + _nbytes((tm, c), F32) + _nbytes(w.shape, BF16)
    return pl.pallas_call(
        functools.partial(_glu_kernel, c=c),
        grid=(t // tm,),
        in_specs=[pl.BlockSpec((tm, d), row), pl.BlockSpec((1, d), fix), pl.BlockSpec(w.shape, fix),
                  pl.BlockSpec((1, 2 * c), fix)],
        out_specs=pl.BlockSpec((tm, c), row),
        out_shape=jax.ShapeDtypeStruct((t, c), F32),
        compiler_params=_params(("parallel",), _vmem_limit(blocks, 0, 4 * _nbytes((tm, c), F32))),
        name="glu",
    )(h, g, w, bias)


def _dwconv_kernel(u_ref, h_ref, dw_ref, db_ref, lg_ref, lb_ref, w_ref, b_ref, o_ref, ext_ref, sh_ref, cv_ref,
                   *, ts, kconv, rc):
    halo = CONV_HALO
    sub = V7X_SUBLANES

    @pl.when(pl.program_id(1) == 0)
    def _():
        ext_ref[0:halo, :] = jnp.zeros((halo, ext_ref.shape[1]), F32)

    ext_ref[halo:halo + ts, :] = u_ref[...]
    base = halo - (kconv - 1)
    for r in range(1, sub):
        sh_ref[r - 1] = ext_ref[pl.ds(r, sh_ref.shape[1]), :]

    def tap(k, r0):
        a, r = divmod(base + k, sub)
        start = pl.multiple_of(r0 + a * sub, sub)
        if r == 0:
            return ext_ref[pl.ds(start, rc), :]
        return sh_ref[r - 1, pl.ds(start, rc), :]

    def chunk(i, c):
        r0 = pl.multiple_of(i * rc, rc)
        acc = db_ref[...] + dw_ref[0:1, :] * tap(0, r0)
        for k in range(1, kconv):
            acc = acc + dw_ref[k:k + 1, :] * tap(k, r0)
        cv_ref[pl.ds(r0, rc), :] = acc
        return c

    lax.fori_loop(0, ts // rc, chunk, 0)
    ext_ref[0:halo, :] = ext_ref[ts:ts + halo, :]

    cv = cv_ref[...]
    mu = jnp.mean(cv, axis=-1, keepdims=True)
    xc = cv - mu
    var = jnp.mean(xc * xc, axis=-1, keepdims=True)
    ln = xc * lax.rsqrt(var + EPS) * lg_ref[...] + lb_ref[...]
    act = _silu(ln).astype(BF16)
    o_ref[...] = h_ref[...] + jnp.dot(act, w_ref[...], preferred_element_type=F32) + b_ref[...]


def _dwconv(u, h, dw_w, dw_b, ln_g, ln_b, w2, b2, b, s):
    t, c = u.shape
    d = h.shape[1]
    kconv = dw_w.shape[0]
    assert kconv - 1 <= CONV_HALO
    ts = _tile(s, CONV_ROW_TILE)
    rc = _tile(ts, CONV_ROW_CHUNK)
    nt = s // ts
    kp = -(-kconv // V7X_SUBLANES) * V7X_SUBLANES
    dwp = jnp.zeros((kp, c), F32).at[:kconv].set(dw_w)
    row = lambda i, j: (i * nt + j, 0)
    fix = lambda i, j: (0, 0)
    blocks = _nbytes((ts, c), F32) + 2 * _nbytes((ts, d), F32) + _nbytes((c, d), BF16) + _nbytes((kp, c), F32)
    sh_rows = ts + CONV_HALO - V7X_SUBLANES
    scratch = _nbytes((ts + CONV_HALO, c), F32) + _nbytes((ts, c), F32) + _nbytes((V7X_SUBLANES - 1, sh_rows, c), F32)
    return pl.pallas_call(
        functools.partial(_dwconv_kernel, ts=ts, kconv=kconv, rc=rc),
        grid=(b, nt),
        in_specs=[pl.BlockSpec((ts, c), row), pl.BlockSpec((ts, d), row), pl.BlockSpec((kp, c), fix),
                  pl.BlockSpec((1, c), fix), pl.BlockSpec((1, c), fix), pl.BlockSpec((1, c), fix),
                  pl.BlockSpec((c, d), fix), pl.BlockSpec((1, d), fix)],
        out_specs=pl.BlockSpec((ts, d), row),
        out_shape=jax.ShapeDtypeStruct((t, d), F32),
        scratch_shapes=[pltpu.VMEM((ts + CONV_HALO, c), F32), pltpu.VMEM((V7X_SUBLANES - 1, sh_rows, c), F32),
                        pltpu.VMEM((ts, c), F32)],
        compiler_params=_params(("parallel", "arbitrary"), _vmem_limit(blocks, scratch, 6 * _nbytes((ts, c), F32))),
        name="dwconv",
    )(u, h, dwp, dw_b.reshape(1, c), ln_g.reshape(1, c), ln_b.reshape(1, c), w2, b2.reshape(1, d))


def _router_kernel(h_ref, g_ref, wr_ref, n_ref, comb_ref, *, ne):
    nf = _rms(h_ref[...], g_ref[...])
    n = nf.astype(BF16)
    n_ref[...] = n
    logits = jnp.dot(n, wr_ref[...], preferred_element_type=F32)
    lane = lax.broadcasted_iota(jnp.int32, logits.shape, 1)
    neg = jnp.float32(-jnp.inf)
    logits = jnp.where(lane < ne, logits, neg)
    big = jnp.int32(V7X_LANES)
    m1 = jnp.max(logits, axis=-1, keepdims=True)
    i1 = jnp.min(jnp.where(logits == m1, lane, big), axis=-1, keepdims=True)
    sel1 = lane == i1
    rest = jnp.where(sel1, neg, logits)
    m2 = jnp.max(rest, axis=-1, keepdims=True)
    i2 = jnp.min(jnp.where(rest == m2, lane, big), axis=-1, keepdims=True)
    sel2 = lane == i2
    e2 = jnp.exp(m2 - m1)
    g1 = 1.0 / (1.0 + e2)
    g2 = e2 / (1.0 + e2)
    comb_ref[...] = jnp.where(sel1, g1, 0.0) + jnp.where(sel2, g2, 0.0)


def _router(h, g, wr_pad, ne):
    t, d = h.shape
    tm = _tile(t, ROW_TILE)
    row = lambda i: (i, 0)
    fix = lambda i: (0, 0)
    blocks = _nbytes((tm, d), F32) + _nbytes((tm, d), BF16) + _nbytes((tm, V7X_LANES), F32) + _nbytes(wr_pad.shape, BF16)
    return pl.pallas_call(
        functools.partial(_router_kernel, ne=ne),
        grid=(t // tm,),
        in_specs=[pl.BlockSpec((tm, d), row), pl.BlockSpec((1, d), fix), pl.BlockSpec(wr_pad.shape, fix)],
        out_specs=[pl.BlockSpec((tm, d), row), pl.BlockSpec((tm, V7X_LANES), row)],
        out_shape=[jax.ShapeDtypeStruct((t, d), BF16), jax.ShapeDtypeStruct((t, V7X_LANES), F32)],
        compiler_params=_params(("parallel",), _vmem_limit(blocks, 0, 2 * _nbytes((tm, d), F32))),
        name="router",
    )(h, g, wr_pad)


def _moe_dense_kernel(h_ref, n_ref, comb_ref, w1_ref, w3_ref, w2_ref, o_ref, acc_ref):
    e = pl.program_id(1)
    f = pl.program_id(2)

    @pl.when((e == 0) & (f == 0))
    def _():
        acc_ref[...] = jnp.zeros_like(acc_ref)

    n = n_ref[...]
    comb = comb_ref[...]
    lane = lax.broadcasted_iota(jnp.int32, comb.shape, 1)
    cw = jnp.sum(jnp.where(lane == e, comb, 0.0), axis=-1, keepdims=True)
    a = _silu(jnp.dot(n, w1_ref[0], preferred_element_type=F32)) * jnp.dot(n, w3_ref[0], preferred_element_type=F32)
    acc_ref[...] += jnp.dot((a * cw).astype(BF16), w2_ref[0], preferred_element_type=F32)

    @pl.when((e == pl.num_programs(1) - 1) & (f == pl.num_programs(2) - 1))
    def _():
        o_ref[...] = h_ref[...] + acc_ref[...]


def _moe_dense(h, n, comb, w1, w3, w2):
    t, d = h.shape
    ne, _, ff = w1.shape
    tm = _tile(t, FFN_ROW_TILE)
    tf = _tile(ff, 2 * FFN_COL_TILE, V7X_LANES)
    blocks = 2 * _nbytes((tm, d), F32) + _nbytes((tm, d), BF16) + _nbytes((tm, V7X_LANES), F32) + 3 * _nbytes((d, tf), BF16)
    scratch = _nbytes((tm, d), F32)
    temps = 4 * _nbytes((tm, tf), F32) + _nbytes((tm, d), F32)
    return pl.pallas_call(
        _moe_dense_kernel,
        grid=(t // tm, ne, ff // tf),
        in_specs=[pl.BlockSpec((tm, d), lambda i, e, f: (i, 0)), pl.BlockSpec((tm, d), lambda i, e, f: (i, 0)),
                  pl.BlockSpec((tm, V7X_LANES), lambda i, e, f: (i, 0)),
                  pl.BlockSpec((1, d, tf), lambda i, e, f: (e, 0, f)), pl.BlockSpec((1, d, tf), lambda i, e, f: (e, 0, f)),
                  pl.BlockSpec((1, tf, d), lambda i, e, f: (e, f, 0))],
        out_specs=pl.BlockSpec((tm, d), lambda i, e, f: (i, 0)),
        out_shape=jax.ShapeDtypeStruct((t, d), F32),
        scratch_shapes=[pltpu.VMEM((tm, d), F32)],
        compiler_params=_params(("parallel", "arbitrary", "arbitrary"), _vmem_limit(blocks, scratch, temps)),
        name="moe_dense",
    )(h, n, comb, w1, w3, w2)


def kernel(x, p, mix_norm_g, ffn_norm_g, ple_norm_g, w_ple, w_ple_gate, ev_w_in, ev_conv_w, ev_conv_b, ev_dt_bias, ev_a_log, ev_d, ev_ssd_norm_g, ev_w_out, ev_ffn_w1, ev_ffn_w3, ev_ffn_w2, od_pw1_w, od_pw1_b, od_dw_w, od_dw_b, od_ln_g, od_ln_b, od_pw2_w, od_pw2_b, od_router_w, od_moe_w1, od_moe_w3, od_moe_w2, final_norm_g):
    b, s, d = x.shape
    t = b * s
    depth = p.shape[0]
    assert depth == 2 and ev_w_in.shape[0] == 1 and od_pw1_w.shape[0] == 1
    sbw = SB_HEADS * SB_HEAD_DIM
    ssdw = SSD_HEADS * SSD_HEAD_DIM
    xbcw = ssdw + 2 * SSD_GROUPS * SSD_STATE
    in_width = 3 * sbw + ssdw + xbcw + SSD_HEADS
    assert ev_w_in.shape[2] == in_width
    bf = lambda w: w.astype(BF16)
    vec = lambda v: v.reshape(1, -1)

    x2 = x.reshape(t, d)
    p2 = p.reshape(depth, t, -1)

    w_in = jnp.zeros((d, in_width - SSD_HEADS + V7X_LANES), BF16).at[:, :in_width].set(bf(ev_w_in[0]))
    qscale = LOG2E / math.sqrt(SB_HEAD_DIM)
    qkv, z, xbc, dt = _in_proj(x2, vec(mix_norm_g[0]), w_in, sbw=sbw, zw=ssdw, xbcw=xbcw, qscale=qscale)
    attn = _sb_attention(qkv, b, s, heads=SB_HEADS, hd=SB_HEAD_DIM)
    ynorm = _ssd(xbc, z, dt, ev_conv_w[0], ev_conv_b[0], ev_dt_bias[0], ev_a_log[0], ev_d[0], ev_ssd_norm_g[0],
                 b, s, heads=SSD_HEADS, hd=SSD_HEAD_DIM, nstate=SSD_STATE)
    h = _out_proj(x2, attn, ynorm, bf(ev_w_out[0]))
    h = _ffn(h, vec(ffn_norm_g[0]), bf(ev_ffn_w1[0]), bf(ev_ffn_w3[0]), bf(ev_ffn_w2[0]))
    h = _ple(h, p2[0], vec(ple_norm_g[0]), bf(w_ple_gate[0]), bf(w_ple[0]))

    u = _glu(h, vec(mix_norm_g[1]), bf(od_pw1_w[0]), vec(od_pw1_b[0]))
    h = _dwconv(u, h, od_dw_w[0], od_dw_b[0], od_ln_g[0], od_ln_b[0], bf(od_pw2_w[0]), od_pw2_b[0], b, s)
    ne = od_router_w.shape[2]
    wr = jnp.zeros((d, V7X_LANES), BF16).at[:, :ne].set(bf(od_router_w[0]))
    n, comb = _router(h, vec(ffn_norm_g[1]), wr, ne)
    h = _moe_dense(h, n, comb, bf(od_moe_w1[0]), bf(od_moe_w3[0]), bf(od_moe_w2[0]))
    out = _ple(h, p2[1], vec(ple_norm_g[1]), bf(w_ple_gate[1]), bf(w_ple[1]), final_g=vec(final_norm_g))
    return out.reshape(b, s, d)
```

```python
import functools
import math

import jax
import jax.numpy as jnp
from jax import lax
from jax.experimental import pallas as pl
from jax.experimental.pallas import tpu as pltpu

F32 = jnp.float32
BF16 = jnp.bfloat16
EPS = 1e-6
LOG2E = 1.4426950408889634

V7X_LANES = 128
V7X_SUBLANES = 8
V7X_VMEM_BYTES = 64 * 1024 * 1024

SB_HEADS = 8
SB_HEAD_DIM = 64
SSD_HEADS = 8
SSD_HEAD_DIM = 64
SSD_STATE = 128
SSD_GROUPS = 2
SSD_CONV = 4
N_EXPERTS = 8
TOP_K = 2

ROW_TILE = 512
FFN_ROW_TILE = 1024
FFN_COL_TILE = 256
ATTN_BLOCK = 256
SSD_CHUNK = 128
CONV_ROW_TILE = 256
CONV_ROW_CHUNK = 32
CONV_HALO = 32
GATHER_ROWS = 512
GATHER_UNROLL = 16


def _tile(n, pref, mult=V7X_SUBLANES):
    if n <= pref:
        return n
    t = (pref // mult) * mult
    while t > mult and n % t:
        t -= mult
    assert n % t == 0, (n, pref, mult)
    return t


def _vmem_limit(block_bytes, scratch_bytes=0, temp_bytes=0):
    need = 2 * block_bytes + scratch_bytes + temp_bytes + (4 << 20)
    return int(min(max(need, 16 << 20), V7X_VMEM_BYTES - (6 << 20)))


def _params(sem, vmem):
    return pltpu.CompilerParams(dimension_semantics=sem, vmem_limit_bytes=vmem)


def _nbytes(shape, dtype):
    return math.prod(shape) * jnp.dtype(dtype).itemsize


def _rms(x, g):
    return x * lax.rsqrt(jnp.mean(x * x, axis=-1, keepdims=True) + EPS) * g


def _sigmoid(x):
    return 1.0 / (1.0 + jnp.exp(-x))


def _silu(x):
    return x * _sigmoid(x)


def _softplus(x):
    return jnp.maximum(x, 0.0) + jnp.log(1.0 + jnp.exp(-jnp.abs(x)))


def _split3(x):
    hi = x.astype(BF16)
    r1 = x - hi.astype(F32)
    mid = r1.astype(BF16)
    lo = (r1 - mid.astype(F32)).astype(BF16)
    return hi, mid, lo


def _dot_exact_rhs(x, m):
    hi, mid, lo = _split3(x)
    d = functools.partial(jnp.dot, preferred_element_type=F32)
    return d(hi, m) + d(mid, m) + d(lo, m)


def _dot_exact_lhs(m, x):
    hi, mid, lo = _split3(x)
    d = functools.partial(jnp.dot, preferred_element_type=F32)
    return d(m, hi) + d(m, mid) + d(m, lo)


def _in_proj_kernel(x_ref, g_ref, w_ref, qkv_ref, z_ref, xbc_ref, dt_ref, *, sbw, zw, xbcw, qscale):
    hn = _rms(x_ref[...], g_ref[...]).astype(BF16)

    def mm(lo, hi):
        return jnp.dot(hn, w_ref[:, lo:hi], preferred_element_type=F32)

    qkv_ref[:, 0:sbw] = (mm(0, sbw) * qscale).astype(BF16)
    qkv_ref[:, sbw:3 * sbw] = mm(sbw, 3 * sbw).astype(BF16)
    o = 3 * sbw
    z_ref[...] = mm(o, o + zw)
    o += zw
    xbc_ref[...] = mm(o, o + xbcw)
    o += xbcw
    dt_ref[...] = mm(o, o + V7X_LANES)


def _in_proj(x2, g, w_pad, *, sbw, zw, xbcw, qscale):
    t, d = x2.shape
    n = w_pad.shape[1]
    tm = _tile(t, ROW_TILE)
    blocks = (_nbytes((tm, d), F32) + _nbytes((d, n), BF16) + _nbytes((tm, 3 * sbw), BF16)
              + _nbytes((tm, zw + xbcw + V7X_LANES), F32))
    return pl.pallas_call(
        functools.partial(_in_proj_kernel, sbw=sbw, zw=zw, xbcw=xbcw, qscale=qscale),
        grid=(t // tm,),
        in_specs=[pl.BlockSpec((tm, d), lambda i: (i, 0)),
                  pl.BlockSpec((1, d), lambda i: (0, 0)),
                  pl.BlockSpec((d, n), lambda i: (0, 0))],
        out_specs=[pl.BlockSpec((tm, 3 * sbw), lambda i: (i, 0)),
                   pl.BlockSpec((tm, zw), lambda i: (i, 0)),
                   pl.BlockSpec((tm, xbcw), lambda i: (i, 0)),
                   pl.BlockSpec((tm, V7X_LANES), lambda i: (i, 0))],
        out_shape=[jax.ShapeDtypeStruct((t, 3 * sbw), BF16),
                   jax.ShapeDtypeStruct((t, zw), F32),
                   jax.ShapeDtypeStruct((t, xbcw), F32),
                   jax.ShapeDtypeStruct((t, V7X_LANES), F32)],
        compiler_params=_params(("parallel",), _vmem_limit(blocks, 0, _nbytes((tm, n), F32))),
        name="in_proj",
    )(x2, g, w_pad)


def _sb_attn_kernel(q_ref, k_ref, v_ref, m_ref, o_ref, acc_ref, car_ref, *, blk, hd):
    qi = pl.program_id(2)
    lane = lax.broadcasted_iota(jnp.int32, (1, 2 * hd), 1)
    q = q_ref[...]
    qz = jnp.zeros_like(q)
    qh = (jnp.where(lane < hd, q, qz), jnp.where(lane >= hd, q, qz))
    acc_ref[...] = jnp.zeros_like(acc_ref)
    car_ref[...] = jnp.zeros_like(car_ref)
    rep = blk // V7X_LANES

    def do_block(j, dmask):
        start = pl.multiple_of(j * blk, blk)
        kblk = k_ref[pl.ds(start, blk), :]
        vblk = v_ref[pl.ds(start, blk), :]
        mm = m_ref[...]
        for h in range(2):
            z2 = lax.dot_general(qh[h], kblk, (((1,), (1,)), ((), ())), preferred_element_type=F32)
            sp2 = jnp.maximum(z2, 0.0) + jnp.log2(1.0 + jnp.exp2(-jnp.abs(z2)))
            if dmask is not None:
                sp2 = jnp.where(dmask, sp2, 0.0)
            ie = jnp.dot(sp2.astype(BF16), mm, preferred_element_type=F32)
            car = car_ref[h]
            w = jnp.exp2(z2 - ie[:, :blk] - jnp.concatenate([car] * rep, axis=1))
            if dmask is not None:
                w = jnp.where(dmask, w, 0.0)
            acc_ref[h] += jnp.dot(w.astype(BF16), vblk, preferred_element_type=F32)
            car_ref[h] = car + ie[:, blk:]

    row = lax.broadcasted_iota(jnp.int32, (blk, blk), 0)
    col = lax.broadcasted_iota(jnp.int32, (blk, blk), 1)
    do_block(qi, col < row)

    def body(i, c):
        do_block(qi - 1 - i, None)
        return c

    lax.fori_loop(0, qi, body, 0)
    o_ref[...] = jnp.where(lane < hd, acc_ref[0], acc_ref[1]).astype(o_ref.dtype)


def _sb_attention(qkv, b, s, *, heads, hd):
    t = qkv.shape[0]
    blk = _tile(s, ATTN_BLOCK, V7X_LANES)
    nq = s // blk
    hp = heads // 2
    width = 2 * hd
    assert width == V7X_LANES
    j = jnp.arange(blk)[:, None]
    c = jnp.arange(blk + V7X_LANES)[None, :]
    m_ext = ((j >= c) | (c >= blk)).astype(BF16)
    blocks = (_nbytes((blk, width), BF16) * 2 + 2 * _nbytes((s, width), BF16)
              + _nbytes(m_ext.shape, BF16))
    scratch = 4 * _nbytes((blk, width), F32)
    temps = 8 * _nbytes((blk, blk + V7X_LANES), F32)
    return pl.pallas_call(
        functools.partial(_sb_attn_kernel, blk=blk, hd=hd),
        grid=(b, hp, nq),
        in_specs=[pl.BlockSpec((blk, width), lambda bi, h, qi: (bi * nq + qi, h)),
                  pl.BlockSpec((s, width), lambda bi, h, qi: (bi, hp + h)),
                  pl.BlockSpec((s, width), lambda bi, h, qi: (bi, 2 * hp + h)),
                  pl.BlockSpec(m_ext.shape, lambda bi, h, qi: (0, 0))],
        out_specs=pl.BlockSpec((blk, width), lambda bi, h, qi: (bi * nq + qi, h)),
        out_shape=jax.ShapeDtypeStruct((t, heads * hd), BF16),
        scratch_shapes=[pltpu.VMEM((2, blk, width), F32), pltpu.VMEM((2, blk, width), F32)],
        compiler_params=_params(("parallel", "parallel", "arbitrary"), _vmem_limit(blocks, scratch, temps)),
        name="sb_attention",
    )(qkv, qkv, qkv, m_ext)


def _ssd_kernel(xbc_ref, z_ref, dt_ref, cw_ref, cb_ref, dtb_ref, alog_ref, dskip_ref, ng_ref, tri_ref, exp_ref,
                o_ref, ext_ref, st_ref, *, q, width, nstate, hd, kconv):
    halo = V7X_SUBLANES
    gw = width // SSD_GROUPS

    @pl.when(pl.program_id(1) == 0)
    def _():
        ext_ref[0:halo, :] = jnp.zeros((halo, ext_ref.shape[1]), F32)
        st_ref[...] = jnp.zeros_like(st_ref)

    x_new = xbc_ref[...]
    ext_ref[halo:halo + q, :] = x_new
    conv = cb_ref[...] + cw_ref[0:1, :] * ext_ref[pl.ds(halo - (kconv - 1), q), :]
    for k in range(1, kconv):
        conv = conv + cw_ref[k:k + 1, :] * ext_ref[pl.ds(halo - (kconv - 1) + k, q), :]
    ext_ref[0:halo, :] = x_new[q - halo:q, :]
    xbc = _silu(conv)
    xs = xbc[:, :width]
    bm = xbc[:, width:width + SSD_GROUPS * nstate]
    cm = xbc[:, width + SSD_GROUPS * nstate:]

    dt = _softplus(dt_ref[...] + dtb_ref[...])
    a = -jnp.exp(alog_ref[...])
    acum = _dot_exact_lhs(tri_ref[...], dt * a)
    ea = jnp.exp(acum)
    dte = jnp.exp(acum[q - 1:q, :] - acum)
    ex = exp_ref[...]
    dt_x = _dot_exact_rhs(dt, ex)
    ea_x = _dot_exact_rhs(ea, ex)
    dte_x = _dot_exact_rhs(dte, ex)
    xdt = xs * dt_x
    xdt_b = xdt.astype(BF16)
    xdec_b = (xdt * dte_x).astype(BF16)
    acum_t = acum.T

    row = lax.broadcasted_iota(jnp.int32, (q, q), 0)
    col = lax.broadcasted_iota(jnp.int32, (q, q), 1)
    causal = col <= row
    lane = lax.broadcasted_iota(jnp.int32, (1, 2 * hd), 1)
    heads_per_group = gw // hd
    y_parts = []
    for g in range(SSD_GROUPS):
        bg = bm[:, g * nstate:(g + 1) * nstate].astype(BF16)
        cg = cm[:, g * nstate:(g + 1) * nstate].astype(BF16)
        gmat = lax.dot_general(cg, bg, (((1,), (1,)), ((), ())), preferred_element_type=F32)
        for pr in range(heads_per_group // 2):
            c0 = g * gw + pr * 2 * hd
            xpair = xdt_b[:, c0:c0 + 2 * hd]
            yd = []
            for hh in range(2):
                h = (c0 // hd) + hh
                seg = acum[:, h:h + 1] - acum_t[h:h + 1, :]
                lmat = jnp.exp(jnp.where(causal, seg, -jnp.inf))
                yd.append(jnp.dot((gmat * lmat).astype(BF16), xpair, preferred_element_type=F32))
            y_parts.append(jnp.where(lane < hd, yd[0], yd[1]))
        hprev = st_ref[:, g * gw:(g + 1) * gw]
        y_off = jnp.dot(cg, hprev.astype(BF16), preferred_element_type=F32) * ea_x[:, g * gw:(g + 1) * gw]
        st_new = lax.dot_general(bg, xdec_b[:, g * gw:(g + 1) * gw], (((0,), (0,)), ((), ())),
                                 preferred_element_type=F32)
        st_ref[:, g * gw:(g + 1) * gw] = hprev * ea_x[q - 1:q, g * gw:(g + 1) * gw] + st_new
        y_parts.append(y_off)
    pp = heads_per_group // 2 + 1
    ys = []
    for g in range(SSD_GROUPS):
        diag = jnp.concatenate(y_parts[g * pp:g * pp + pp - 1], axis=1)
        ys.append(diag + y_parts[g * pp + pp - 1])
    y = jnp.concatenate(ys, axis=1) + xs * dskip_ref[...]
    o_ref[...] = _rms(y * _silu(z_ref[...]), ng_ref[...]).astype(o_ref.dtype)


def _ssd(xbc, z, dt, conv_w, conv_b, dt_bias, a_log, d_skip, norm_g, b, s, *, heads, hd, nstate):
    t, xbcw = xbc.shape
    width = heads * hd
    q = _tile(s, SSD_CHUNK, V7X_LANES)
    nc = s // q
    kconv = conv_w.shape[0]
    cw = jnp.zeros((V7X_SUBLANES, xbcw), F32).at[:kconv].set(conv_w)
    pad = lambda v: jnp.zeros((1, V7X_LANES), F32).at[0, :heads].set(v)
    tri = (jnp.arange(q)[None, :] <= jnp.arange(q)[:, None]).astype(BF16)
    hx = (jnp.arange(V7X_LANES)[:, None] == (jnp.arange(width)[None, :] // hd)).astype(BF16)
    row = lambda i, c: (i * nc + c, 0)
    fix = lambda i, c: (0, 0)
    blocks = (_nbytes((q, xbcw + width + V7X_LANES), F32) + _nbytes((q, width), BF16)
              + _nbytes((q, q), BF16) + _nbytes((V7X_LANES, width), BF16) + 8 * _nbytes((1, xbcw), F32))
    scratch = _nbytes((q + V7X_SUBLANES, xbcw), F32) + _nbytes((nstate, width), F32)
    temps = 24 * _nbytes((q, xbcw), F32)
    return pl.pallas_call(
        functools.partial(_ssd_kernel, q=q, width=width, nstate=nstate, hd=hd, kconv=kconv),
        grid=(b, nc),
        in_specs=[pl.BlockSpec((q, xbcw), row), pl.BlockSpec((q, width), row), pl.BlockSpec((q, V7X_LANES), row),
                  pl.BlockSpec((V7X_SUBLANES, xbcw), fix), pl.BlockSpec((1, xbcw), fix),
                  pl.BlockSpec((1, V7X_LANES), fix), pl.BlockSpec((1, V7X_LANES), fix),
                  pl.BlockSpec((1, width), fix), pl.BlockSpec((1, width), fix),
                  pl.BlockSpec((q, q), fix), pl.BlockSpec((V7X_LANES, width), fix)],
        out_specs=pl.BlockSpec((q, width), row),
        out_shape=jax.ShapeDtypeStruct((t, width), BF16),
        scratch_shapes=[pltpu.VMEM((q + V7X_SUBLANES, xbcw), F32), pltpu.VMEM((nstate, width), F32)],
        compiler_params=_params(("parallel", "arbitrary"), _vmem_limit(blocks, scratch, temps)),
        name="ssd",
    )(xbc, z, dt, cw, conv_b.reshape(1, xbcw), pad(dt_bias), pad(a_log),
      jnp.repeat(d_skip, hd).reshape(1, width), norm_g.reshape(1, width), tri, hx)


def _out_proj_kernel(x_ref, a_ref, y_ref, w_ref, o_ref, *, wa):
    acc = jnp.dot(a_ref[...], w_ref[0:wa, :], preferred_element_type=F32)
    acc = acc + jnp.dot(y_ref[...], w_ref[wa:, :], preferred_element_type=F32)
    o_ref[...] = x_ref[...] + acc


def _out_proj(x2, attn, ynorm, w):
    t, d = x2.shape
    wa, wy = attn.shape[1], ynorm.shape[1]
    tm = _tile(t, ROW_TILE)
    row = lambda i: (i, 0)
    blocks = 2 * _nbytes((tm, d), F32) + _nbytes((tm, wa + wy), BF16) + _nbytes(w.shape, BF16)
    return pl.pallas_call(
        functools.partial(_out_proj_kernel, wa=wa),
        grid=(t // tm,),
        in_specs=[pl.BlockSpec((tm, d), row), pl.BlockSpec((tm, wa), row), pl.BlockSpec((tm, wy), row),
                  pl.BlockSpec(w.shape, lambda i: (0, 0))],
        out_specs=pl.BlockSpec((tm, d), row),
        out_shape=jax.ShapeDtypeStruct((t, d), F32),
        compiler_params=_params(("parallel",), _vmem_limit(blocks, 0, 2 * _nbytes((tm, d), F32))),
        name="out_proj",
    )(x2, attn, ynorm, w)


def _ffn_kernel(h_ref, g_ref, w1_ref, w3_ref, w2_ref, o_ref, n_ref, acc_ref):
    f = pl.program_id(1)

    @pl.when(f == 0)
    def _():
        n_ref[...] = _rms(h_ref[...], g_ref[...]).astype(BF16)
        acc_ref[...] = jnp.zeros_like(acc_ref)

    n = n_ref[...]
    a = _silu(jnp.dot(n, w1_ref[...], preferred_element_type=F32)) * jnp.dot(n, w3_ref[...], preferred_element_type=F32)
    acc_ref[...] += jnp.dot(a.astype(BF16), w2_ref[...], preferred_element_type=F32)

    @pl.when(f == pl.num_programs(1) - 1)
    def _():
        o_ref[...] = h_ref[...] + acc_ref[...]


def _ffn(h, g, w1, w3, w2):
    t, d = h.shape
    ff = w1.shape[1]
    tm = _tile(t, FFN_ROW_TILE)
    tf = _tile(ff, FFN_COL_TILE, V7X_LANES)
    blocks = 2 * _nbytes((tm, d), F32) + 3 * _nbytes((d, tf), BF16)
    scratch = _nbytes((tm, d), BF16) + _nbytes((tm, d), F32)
    temps = 4 * _nbytes((tm, tf), F32) + _nbytes((tm, d), F32)
    return pl.pallas_call(
        _ffn_kernel,
        grid=(t // tm, ff // tf),
        in_specs=[pl.BlockSpec((tm, d), lambda i, f: (i, 0)), pl.BlockSpec((1, d), lambda i, f: (0, 0)),
                  pl.BlockSpec((d, tf), lambda i, f: (0, f)), pl.BlockSpec((d, tf), lambda i, f: (0, f)),
                  pl.BlockSpec((tf, d), lambda i, f: (f, 0))],
        out_specs=pl.BlockSpec((tm, d), lambda i, f: (i, 0)),
        out_shape=jax.ShapeDtypeStruct((t, d), F32),
        scratch_shapes=[pltpu.VMEM((tm, d), BF16), pltpu.VMEM((tm, d), F32)],
        compiler_params=_params(("parallel", "arbitrary"), _vmem_limit(blocks, scratch, temps)),
        name="ffn",
    )(h, g, w1, w3, w2)


def _ple_kernel(h_ref, p_ref, g_ref, wg_ref, wp_ref, o_ref):
    h = h_ref[...]
    n = _rms(h, g_ref[...]).astype(BF16)
    gate = _sigmoid(jnp.dot(n, wg_ref[...], preferred_element_type=F32))
    emb = jnp.dot(p_ref[...].astype(BF16), wp_ref[...], preferred_element_type=F32)
    o_ref[...] = h + emb * gate


def _ple(h, p2, g, w_gate, w_ple):
    t, d = h.shape
    pd = p2.shape[1]
    tm = _tile(t, ROW_TILE)
    row = lambda i: (i, 0)
    fix = lambda i: (0, 0)
    blocks = 2 * _nbytes((tm, d), F32) + _nbytes((tm, pd), F32) + _nbytes((d + pd, d), BF16)
    return pl.pallas_call(
        _ple_kernel,
        grid=(t // tm,),
        in_specs=[pl.BlockSpec((tm, d), row), pl.BlockSpec((tm, pd), row), pl.BlockSpec((1, d), fix),
                  pl.BlockSpec((d, d), fix), pl.BlockSpec((pd, d), fix)],
        out_specs=pl.BlockSpec((tm, d), row),
        out_shape=jax.ShapeDtypeStruct((t, d), F32),
        compiler_params=_params(("parallel",), _vmem_limit(blocks, 0, 4 * _nbytes((tm, d), F32))),
        name="ple",
    )(h, p2, g, w_gate, w_ple)


def _glu_kernel(h_ref, g_ref, w_ref, b_ref, o_ref, *, c):
    n = _rms(h_ref[...], g_ref[...]).astype(BF16)
    val = jnp.dot(n, w_ref[:, 0:c], preferred_element_type=F32) + b_ref[:, 0:c]
    gate = jnp.dot(n, w_ref[:, c:], preferred_element_type=F32) + b_ref[:, c:]
    o_ref[...] = val * _sigmoid(gate)


def _glu(h, g, w, bias):
    t, d = h.shape
    c = w.shape[1] // 2
    tm = _tile(t, ROW_TILE)
    row = lambda i: (i, 0)
    fix = lambda i: (0, 0)
    blocks = _nbytes((tm, d), F32) + _nbytes((tm, c), F32) + _nbytes(w.shape, BF16)
    return pl.pallas_call(
        functools.partial(_glu_kernel, c=c),
        grid=(t // tm,),
        in_specs=[pl.BlockSpec((tm, d), row), pl.BlockSpec((1, d), fix), pl.BlockSpec(w.shape, fix),
                  pl.BlockSpec((1, 2 * c), fix)],
        out_specs=pl.BlockSpec((tm, c), row),
        out_shape=jax.ShapeDtypeStruct((t, c), F32),
        compiler_params=_params(("parallel",), _vmem_limit(blocks, 0, 4 * _nbytes((tm, c), F32))),
        name="glu",
    )(h, g, w, bias)


def _dwconv_kernel(u_ref, h_ref, dw_ref, db_ref, lg_ref, lb_ref, w_ref, b_ref, o_ref, ext_ref, sh_ref, cv_ref,
                   *, ts, kconv, rc):
    halo = CONV_HALO
    sub = V7X_SUBLANES

    @pl.when(pl.program_id(1) == 0)
    def _():
        ext_ref[0:halo, :] = jnp.zeros((halo, ext_ref.shape[1]), F32)

    ext_ref[halo:halo + ts, :] = u_ref[...]
    base = halo - (kconv - 1)
    for r in range(1, sub):
        sh_ref[r - 1] = ext_ref[pl.ds(r, sh_ref.shape[1]), :]

    def tap(k, r0):
        a, r = divmod(base + k, sub)
        start = pl.multiple_of(r0 + a * sub, sub)
        if r == 0:
            return ext_ref[pl.ds(start, rc), :]
        return sh_ref[r - 1, pl.ds(start, rc), :]

    def chunk(i, c):
        r0 = pl.multiple_of(i * rc, rc)
        acc = db_ref[...] + dw_ref[0:1, :] * tap(0, r0)
        for k in range(1, kconv):
            acc = acc + dw_ref[k:k + 1, :] * tap(k, r0)
        cv_ref[pl.ds(r0, rc), :] = acc
        return c

    lax.fori_loop(0, ts // rc, chunk, 0)
    ext_ref[0:halo, :] = ext_ref[ts:ts + halo, :]

    cv = cv_ref[...]
    mu = jnp.mean(cv, axis=-1, keepdims=True)
    xc = cv - mu
    var = jnp.mean(xc * xc, axis=-1, keepdims=True)
    ln = xc * lax.rsqrt(var + EPS) * lg_ref[...] + lb_ref[...]
    act = _silu(ln).astype(BF16)
    o_ref[...] = h_ref[...] + jnp.dot(act, w_ref[...], preferred_element_type=F32) + b_ref[...]


def _dwconv(u, h, dw_w, dw_b, ln_g, ln_b, w2, b2, b, s):
    t, c = u.shape
    d = h.shape[1]
    kconv = dw_w.shape[0]
    assert kconv - 1 <= CONV_HALO
    ts = _tile(s, CONV_ROW_TILE)
    rc = _tile(ts, CONV_ROW_CHUNK)
    nt = s // ts
    kp = -(-kconv // V7X_SUBLANES) * V7X_SUBLANES
    dwp = jnp.zeros((kp, c), F32).at[:kconv].set(dw_w)
    row = lambda i, j: (i * nt + j, 0)
    fix = lambda i, j: (0, 0)
    blocks = _nbytes((ts, c), F32) + 2 * _nbytes((ts, d), F32) + _nbytes((c, d), BF16) + _nbytes((kp, c), F32)
    sh_rows = ts + CONV_HALO - V7X_SUBLANES
    scratch = _nbytes((ts + CONV_HALO, c), F32) + _nbytes((ts, c), F32) + _nbytes((V7X_SUBLANES - 1, sh_rows, c), F32)
    return pl.pallas_call(
        functools.partial(_dwconv_kernel, ts=ts, kconv=kconv, rc=rc),
        grid=(b, nt),
        in_specs=[pl.BlockSpec((ts, c), row), pl.BlockSpec((ts, d), row), pl.BlockSpec((kp, c), fix),
                  pl.BlockSpec((1, c), fix), pl.BlockSpec((1, c), fix), pl.BlockSpec((1, c), fix),
                  pl.BlockSpec((c, d), fix), pl.BlockSpec((1, d), fix)],
        out_specs=pl.BlockSpec((ts, d), row),
        out_shape=jax.ShapeDtypeStruct((t, d), F32),
        scratch_shapes=[pltpu.VMEM((ts + CONV_HALO, c), F32), pltpu.VMEM((V7X_SUBLANES - 1, sh_rows, c), F32),
                        pltpu.VMEM((ts, c), F32)],
        compiler_params=_params(("parallel", "arbitrary"), _vmem_limit(blocks, scratch, 6 * _nbytes((ts, c), F32))),
        name="dwconv",
    )(u, h, dwp, dw_b.reshape(1, c), ln_g.reshape(1, c), ln_b.reshape(1, c), w2, b2.reshape(1, d))


def _router_kernel(h_ref, g_ref, wr_ref, n_ref, meta_ref, *, ne):
    nf = _rms(h_ref[...], g_ref[...])
    n_ref[...] = nf
    logits = jnp.dot(nf.astype(BF16), wr_ref[...], preferred_element_type=F32)
    lane = lax.broadcasted_iota(jnp.int32, logits.shape, 1)
    neg = jnp.float32(-jnp.inf)
    logits = jnp.where(lane < ne, logits, neg)
    big = jnp.int32(V7X_LANES)
    m1 = jnp.max(logits, axis=-1, keepdims=True)
    i1 = jnp.min(jnp.where(logits == m1, lane, big), axis=-1, keepdims=True)
    sel1 = lane == i1
    rest = jnp.where(sel1, neg, logits)
    m2 = jnp.max(rest, axis=-1, keepdims=True)
    i2 = jnp.min(jnp.where(rest == m2, lane, big), axis=-1, keepdims=True)
    e2 = jnp.exp(m2 - m1)
    g1 = 1.0 / (1.0 + e2)
    g2 = e2 / (1.0 + e2)
    meta = jnp.where(lane == 0, g1, jnp.where(lane == 1, g2, 0.0))
    meta = jnp.where(lane == 2, i1.astype(F32), jnp.where(lane == 3, i2.astype(F32), meta))
    meta_ref[...] = meta


def _router(h, g, wr_pad, ne):
    t, d = h.shape
    tm = _tile(t, ROW_TILE)
    row = lambda i: (i, 0)
    fix = lambda i: (0, 0)
    blocks = 2 * _nbytes((tm, d), F32) + _nbytes((tm, V7X_LANES), F32) + _nbytes(wr_pad.shape, BF16)
    return pl.pallas_call(
        functools.partial(_router_kernel, ne=ne),
        grid=(t // tm,),
        in_specs=[pl.BlockSpec((tm, d), row), pl.BlockSpec((1, d), fix), pl.BlockSpec(wr_pad.shape, fix)],
        out_specs=[pl.BlockSpec((tm, d), row), pl.BlockSpec((tm, V7X_LANES), row)],
        out_shape=[jax.ShapeDtypeStruct((t, d), F32), jax.ShapeDtypeStruct((t, V7X_LANES), F32)],
        compiler_params=_params(("parallel",), _vmem_limit(blocks, 0, 2 * _nbytes((tm, d), F32))),
        name="router",
    )(h, g, wr_pad)


def _moe_plan(meta, tm, ne):
    t = meta.shape[0]
    na = TOP_K * t
    nt = na // tm + ne
    e_flat = meta[:, 2:2 + TOP_K].astype(jnp.int32).reshape(na)
    onehot = (e_flat[:, None] == jnp.arange(ne, dtype=jnp.int32)[None, :]).astype(jnp.int32)
    csum = jnp.cumsum(onehot, axis=0)
    counts = csum[-1]
    rank = jnp.sum(csum * onehot, axis=1) - 1
    padded = ((counts + tm - 1) // tm) * tm
    gend = jnp.cumsum(padded)
    gstart = gend - padded
    slot = gstart[e_flat] + rank
    src = jnp.zeros((nt * tm,), jnp.int32).at[slot].set(jnp.arange(na, dtype=jnp.int32) // TOP_K)
    tile_start = jnp.arange(nt, dtype=jnp.int32) * tm
    tile_e = jnp.sum((tile_start[:, None] >= gend[None, :]).astype(jnp.int32), axis=1)
    valid = (tile_e < ne).astype(jnp.int32)
    return slot, src, jnp.minimum(tile_e, ne - 1), valid


def _row_gather_kernel(idx_ref, tab_ref, out_ref, sem, *, g, cols, pair):
    i = pl.program_id(0)
    unroll = GATHER_UNROLL

    def copy(u, k):
        row = idx_ref[0, 0, u * unroll + k]
        if pair:
            dst = out_ref.at[pl.ds(i * (g // 2) + u * (unroll // 2) + k // 2, 1), pl.ds((k % 2) * cols, cols)]
        else:
            dst = out_ref.at[pl.ds(i * g + u * unroll + k, 1), :]
        return pltpu.make_async_copy(tab_ref.at[pl.ds(row, 1), :], dst, sem)

    def start(u, c):
        for k in range(unroll):
            copy(u, k).start()
        return c

    def wait(u, c):
        for k in range(unroll):
            copy(u, k).wait()
        return c

    lax.fori_loop(0, g // unroll, start, 0)
    lax.fori_loop(0, g // unroll, wait, 0)


def _row_gather(table, idx, *, pair):
    n = idx.shape[0]
    cols = table.shape[1]
    g = _tile(n, GATHER_ROWS, GATHER_UNROLL)
    assert n % g == 0 and g % GATHER_UNROLL == 0 and GATHER_UNROLL % 2 == 0
    out_shape = (n // 2, 2 * cols) if pair else (n, cols)
    return pl.pallas_call(
        functools.partial(_row_gather_kernel, g=g, cols=cols, pair=pair),
        grid=(n // g,),
        in_specs=[pl.BlockSpec((1, 1, g), lambda i: (i, 0, 0), memory_space=pltpu.SMEM),
                  pl.BlockSpec(memory_space=pl.ANY)],
        out_specs=pl.BlockSpec(memory_space=pl.ANY),
        out_shape=jax.ShapeDtypeStruct(out_shape, table.dtype),
        scratch_shapes=[pltpu.SemaphoreType.DMA(())],
        compiler_params=_params(("arbitrary",), 16 << 20),
        name="combine_gather" if pair else "dispatch_gather",
    )(idx.reshape(n // g, 1, g), table)


def _moe_kernel(te_ref, tv_ref, x_ref, w1_ref, w3_ref, w2_ref, o_ref, n_ref, acc_ref):
    i = pl.program_id(0)
    f = pl.program_id(1)
    last = pl.num_programs(1) - 1
    valid = tv_ref[i] == 1

    @pl.when(valid & (f == 0))
    def _():
        n_ref[...] = x_ref[...].astype(BF16)
        acc_ref[...] = jnp.zeros_like(acc_ref)

    @pl.when(valid)
    def _():
        n = n_ref[...]
        a = _silu(jnp.dot(n, w1_ref[0], preferred_element_type=F32)) * jnp.dot(n, w3_ref[0], preferred_element_type=F32)
        acc_ref[...] += jnp.dot(a.astype(BF16), w2_ref[0], preferred_element_type=F32)

    @pl.when(valid & (f == last))
    def _():
        o_ref[...] = acc_ref[...]

    @pl.when(jnp.logical_not(valid) & (f == last))
    def _():
        o_ref[...] = jnp.zeros_like(o_ref)


def _moe_experts(xs, tile_e, tile_valid, w1, w3, w2, tm):
    ns, d = xs.shape
    ne, _, ff = w1.shape
    tf = _tile(ff, 2 * FFN_COL_TILE, V7X_LANES)
    nf = ff // tf
    fidx = lambda f, tv, i: f * tv[i] + (nf - 1) * (1 - tv[i])
    blocks = 2 * _nbytes((tm, d), F32) + 3 * _nbytes((d, tf), BF16)
    scratch = _nbytes((tm, d), BF16) + _nbytes((tm, d), F32)
    temps = 4 * _nbytes((tm, tf), F32) + _nbytes((tm, d), F32)
    grid_spec = pltpu.PrefetchScalarGridSpec(
        num_scalar_prefetch=2,
        grid=(ns // tm, nf),
        in_specs=[pl.BlockSpec((tm, d), lambda i, f, te, tv: (i, 0)),
                  pl.BlockSpec((1, d, tf), lambda i, f, te, tv: (te[i], 0, fidx(f, tv, i))),
                  pl.BlockSpec((1, d, tf), lambda i, f, te, tv: (te[i], 0, fidx(f, tv, i))),
                  pl.BlockSpec((1, tf, d), lambda i, f, te, tv: (te[i], fidx(f, tv, i), 0))],
        out_specs=pl.BlockSpec((tm, d), lambda i, f, te, tv: (i, 0)),
        scratch_shapes=[pltpu.VMEM((tm, d), BF16), pltpu.VMEM((tm, d), F32)])
    return pl.pallas_call(
        _moe_kernel,
        grid_spec=grid_spec,
        out_shape=jax.ShapeDtypeStruct((ns, d), F32),
        compiler_params=_params(("arbitrary", "arbitrary"), _vmem_limit(blocks, scratch, temps)),
        name="moe_experts",
    )(tile_e, tile_valid, xs, w1, w3, w2)


def _combine_ple_kernel(h_ref, y_ref, m_ref, p_ref, g_ref, wg_ref, wp_ref, fg_ref, o_ref, *, d):
    meta = m_ref[...]
    h = h_ref[...] + meta[:, 0:1] * y_ref[:, 0:d] + meta[:, 1:2] * y_ref[:, d:2 * d]
    n = _rms(h, g_ref[...]).astype(BF16)
    gate = _sigmoid(jnp.dot(n, wg_ref[...], preferred_element_type=F32))
    emb = jnp.dot(p_ref[...].astype(BF16), wp_ref[...], preferred_element_type=F32)
    o_ref[...] = _rms(h + emb * gate, fg_ref[...])


def _combine_ple(h, yg, meta, p2, g, w_gate, w_ple, final_g):
    t, d = h.shape
    pd = p2.shape[1]
    tm = _tile(t, ROW_TILE)
    row = lambda i: (i, 0)
    fix = lambda i: (0, 0)
    blocks = (4 * _nbytes((tm, d), F32) + _nbytes((tm, pd + V7X_LANES), F32) + _nbytes((d + pd, d), BF16))
    return pl.pallas_call(
        functools.partial(_combine_ple_kernel, d=d),
        grid=(t // tm,),
        in_specs=[pl.BlockSpec((tm, d), row), pl.BlockSpec((tm, 2 * d), row), pl.BlockSpec((tm, V7X_LANES), row),
                  pl.BlockSpec((tm, pd), row), pl.BlockSpec((1, d), fix),
                  pl.BlockSpec((d, d), fix), pl.BlockSpec((pd, d), fix), pl.BlockSpec((1, d), fix)],
        out_specs=pl.BlockSpec((tm, d), row),
        out_shape=jax.ShapeDtypeStruct((t, d), F32),
        compiler_params=_params(("parallel",), _vmem_limit(blocks, 0, 4 * _nbytes((tm, d), F32))),
        name="combine_ple_final",
    )(h, yg, meta, p2, g, w_gate, w_ple, final_g)


def kernel(x, p, mix_norm_g, ffn_norm_g, ple_norm_g, w_ple, w_ple_gate, ev_w_in, ev_conv_w, ev_conv_b, ev_dt_bias, ev_a_log, ev_d, ev_ssd_norm_g, ev_w_out, ev_ffn_w1, ev_ffn_w3, ev_ffn_w2, od_pw1_w, od_pw1_b, od_dw_w, od_dw_b, od_ln_g, od_ln_b, od_pw2_w, od_pw2_b, od_router_w, od_moe_w1, od_moe_w3, od_moe_w2, final_norm_g):
    b, s, d = x.shape
    t = b * s
    depth = p.shape[0]
    assert depth == 2 and ev_w_in.shape[0] == 1 and od_pw1_w.shape[0] == 1
    sbw = SB_HEADS * SB_HEAD_DIM
    ssdw = SSD_HEADS * SSD_HEAD_DIM
    xbcw = ssdw + 2 * SSD_GROUPS * SSD_STATE
    in_width = 3 * sbw + ssdw + xbcw + SSD_HEADS
    assert ev_w_in.shape[2] == in_width
    bf = lambda w: w.astype(BF16)
    vec = lambda v: v.reshape(1, -1)

    x2 = x.reshape(t, d)
    p2 = p.reshape(depth, t, -1)

    w_in = jnp.zeros((d, in_width - SSD_HEADS + V7X_LANES), BF16).at[:, :in_width].set(bf(ev_w_in[0]))
    qscale = LOG2E / math.sqrt(SB_HEAD_DIM)
    qkv, z, xbc, dt = _in_proj(x2, vec(mix_norm_g[0]), w_in, sbw=sbw, zw=ssdw, xbcw=xbcw, qscale=qscale)
    attn = _sb_attention(qkv, b, s, heads=SB_HEADS, hd=SB_HEAD_DIM)
    ynorm = _ssd(xbc, z, dt, ev_conv_w[0], ev_conv_b[0], ev_dt_bias[0], ev_a_log[0], ev_d[0], ev_ssd_norm_g[0],
                 b, s, heads=SSD_HEADS, hd=SSD_HEAD_DIM, nstate=SSD_STATE)
    h = _out_proj(x2, attn, ynorm, bf(ev_w_out[0]))
    h = _ffn(h, vec(ffn_norm_g[0]), bf(ev_ffn_w1[0]), bf(ev_ffn_w3[0]), bf(ev_ffn_w2[0]))
    h = _ple(h, p2[0], vec(ple_norm_g[0]), bf(w_ple_gate[0]), bf(w_ple[0]))

    u = _glu(h, vec(mix_norm_g[1]), bf(od_pw1_w[0]), vec(od_pw1_b[0]))
    h = _dwconv(u, h, od_dw_w[0], od_dw_b[0], od_ln_g[0], od_ln_b[0], bf(od_pw2_w[0]), od_pw2_b[0], b, s)
    ne = od_router_w.shape[2]
    wr = jnp.zeros((d, V7X_LANES), BF16).at[:, :ne].set(bf(od_router_w[0]))
    n, meta = _router(h, vec(ffn_norm_g[1]), wr, ne)
    tm = _tile(TOP_K * t, FFN_ROW_TILE)
    slot, src, tile_e, tile_valid = _moe_plan(meta, tm, ne)
    xs = _row_gather(n, src, pair=False)
    ys = _moe_experts(xs, tile_e, tile_valid, bf(od_moe_w1[0]), bf(od_moe_w3[0]), bf(od_moe_w2[0]), tm)
    yg = _row_gather(ys, slot, pair=True)
    out = _combine_ple(h, yg, meta, p2[1], vec(ple_norm_g[1]), bf(w_ple_gate[1]), bf(w_ple[1]), vec(final_norm_g))
    return out.reshape(b, s, d)
```

```python
import functools
import math

import jax
import jax.numpy as jnp
from jax import lax
from jax.experimental import pallas as pl
from jax.experimental.pallas import tpu as pltpu
from jax.experimental.pallas import tpu_sc as plsc

F32 = jnp.float32
BF16 = jnp.bfloat16
EPS = 1e-6
LOG2E = 1.4426950408889634

V7X_LANES = 128
V7X_SUBLANES = 8
V7X_VMEM_BYTES = 64 * 1024 * 1024

SB_HEADS = 8
SB_HEAD_DIM = 64
SSD_HEADS = 8
SSD_HEAD_DIM = 64
SSD_STATE = 128
SSD_GROUPS = 2
SSD_CONV = 4
N_EXPERTS = 8
TOP_K = 2

ROW_TILE = 512
FFN_ROW_TILE = 1024
FFN_COL_TILE = 1408
MOE_COL_TILE = 896
ATTN_BLOCK = 256
ATTN_PAIRS = 2
SSD_CHUNK = 128
CONV_ROW_TILE = 256
CONV_ROW_CHUNK = 32
CONV_HALO = 32
SC_GATHER_ROWS = 32


def _tile(n, pref, mult=V7X_SUBLANES):
    if n <= pref:
        return n
    t = (pref // mult) * mult
    while t > mult and n % t:
        t -= mult
    assert n % t == 0, (n, pref, mult)
    return t


def _vmem_limit(block_bytes, scratch_bytes=0, temp_bytes=0):
    need = 2 * block_bytes + scratch_bytes + temp_bytes + (4 << 20)
    return int(min(max(need, 16 << 20), V7X_VMEM_BYTES - (6 << 20)))


def _params(sem, vmem):
    return pltpu.CompilerParams(dimension_semantics=sem, vmem_limit_bytes=vmem)


def _nbytes(shape, dtype):
    return math.prod(shape) * jnp.dtype(dtype).itemsize


def _rms(x, g):
    return x * lax.rsqrt(jnp.mean(x * x, axis=-1, keepdims=True) + EPS) * g


def _sigmoid(x):
    return 1.0 / (1.0 + jnp.exp(-x))


def _silu(x):
    return x * _sigmoid(x)


def _softplus(x):
    return jnp.maximum(x, 0.0) + jnp.log(1.0 + jnp.exp(-jnp.abs(x)))


def _split3(x):
    hi = x.astype(BF16)
    r1 = x - hi.astype(F32)
    mid = r1.astype(BF16)
    lo = (r1 - mid.astype(F32)).astype(BF16)
    return hi, mid, lo


def _dot_exact_rhs(x, m):
    hi, mid, lo = _split3(x)
    d = functools.partial(jnp.dot, preferred_element_type=F32)
    return d(hi, m) + d(mid, m) + d(lo, m)


def _dot_exact_lhs(m, x):
    hi, mid, lo = _split3(x)
    d = functools.partial(jnp.dot, preferred_element_type=F32)
    return d(m, hi) + d(m, mid) + d(m, lo)


def _in_proj_kernel(x_ref, g_ref, w_ref, wvt_ref, qk_ref, vt_ref, z_ref, xbc_ref, dt_ref, *, sbw, zw, xbcw, qscale):
    hn = _rms(x_ref[...], g_ref[...]).astype(BF16)

    def mm(lo, hi):
        return jnp.dot(hn, w_ref[:, lo:hi], preferred_element_type=F32)

    qk_ref[:, 0:sbw] = (mm(0, sbw) * qscale).astype(BF16)
    qk_ref[:, sbw:2 * sbw] = mm(sbw, 2 * sbw).astype(BF16)
    vt_ref[...] = lax.dot_general(wvt_ref[...], hn, (((1,), (1,)), ((), ())),
                                  preferred_element_type=F32).astype(BF16)
    o = 2 * sbw
    z_ref[...] = mm(o, o + zw)
    o += zw
    xbc_ref[...] = mm(o, o + xbcw)
    o += xbcw
    dt_ref[...] = mm(o, o + V7X_LANES)


def _in_proj(x2, g, w_pad, w_vt, *, sbw, zw, xbcw, qscale):
    t, d = x2.shape
    n = w_pad.shape[1]
    tm = _tile(t, ROW_TILE, V7X_LANES)
    blocks = (_nbytes((tm, d), F32) + _nbytes((d, n + sbw), BF16) + _nbytes((tm, 3 * sbw), BF16)
              + _nbytes((tm, zw + xbcw + V7X_LANES), F32))
    return pl.pallas_call(
        functools.partial(_in_proj_kernel, sbw=sbw, zw=zw, xbcw=xbcw, qscale=qscale),
        grid=(t // tm,),
        in_specs=[pl.BlockSpec((tm, d), lambda i: (i, 0)),
                  pl.BlockSpec((1, d), lambda i: (0, 0)),
                  pl.BlockSpec((d, n), lambda i: (0, 0)),
                  pl.BlockSpec((sbw, d), lambda i: (0, 0))],
        out_specs=[pl.BlockSpec((tm, 2 * sbw), lambda i: (i, 0)),
                   pl.BlockSpec((sbw, tm), lambda i: (0, i)),
                   pl.BlockSpec((tm, zw), lambda i: (i, 0)),
                   pl.BlockSpec((tm, xbcw), lambda i: (i, 0)),
                   pl.BlockSpec((tm, V7X_LANES), lambda i: (i, 0))],
        out_shape=[jax.ShapeDtypeStruct((t, 2 * sbw), BF16),
                   jax.ShapeDtypeStruct((sbw, t), BF16),
                   jax.ShapeDtypeStruct((t, zw), F32),
                   jax.ShapeDtypeStruct((t, xbcw), F32),
                   jax.ShapeDtypeStruct((t, V7X_LANES), F32)],
        compiler_params=_params(("parallel",), _vmem_limit(blocks, 0, _nbytes((tm, n), F32))),
        name="in_proj",
    )(x2, g, w_pad, w_vt)


ATTN_MASKED = -1e30


def _sb_attn_kernel(q_ref, k_ref, vt_ref, m_ref, o_ref, *scratch, blk, hd, npair):
    qi = pl.program_id(2)
    nblocks = qi + 1
    width = 2 * hd
    nh = 2 * npair
    acc_refs, car_refs, z_refs, incl_refs = (scratch[i * nh:(i + 1) * nh] for i in range(4))
    lane = lax.broadcasted_iota(jnp.int32, (1, width), 1)
    qh = []
    for p in range(npair):
        q = q_ref[:, p * width:(p + 1) * width]
        qz = jnp.zeros_like(q)
        qh += [jnp.where(lane < hd, q, qz), jnp.where(lane >= hd, q, qz)]
    sign = jnp.uint32(0x80000000)

    def key_start(m):
        j = jnp.where(m < nblocks, qi - m, qi)
        return pl.multiple_of(j * blk, blk)

    def scores(m, h):
        p = h // 2
        kblk = k_ref[pl.ds(key_start(m), blk), p * width:(p + 1) * width]
        return lax.dot_general(kblk, qh[h], (((1,), (1,)), ((), ())), preferred_element_type=F32)

    row = lax.broadcasted_iota(jnp.int32, (blk, blk), 0)
    col = lax.broadcasted_iota(jnp.int32, (blk, blk), 1)
    for h in range(nh):
        z_refs[h][0] = jnp.where(row < col, scores(0, h), ATTN_MASKED)
        z_refs[h][2] = jnp.full((blk, blk), ATTN_MASKED, F32)
        incl_refs[h][1] = jnp.zeros((blk, blk), F32)
        acc_refs[h][...] = jnp.zeros_like(acc_refs[h])
        car_refs[h][...] = jnp.zeros_like(car_refs[h])

    def body(m, c):
        s_next = lax.rem(m + 1, 3)
        s_cur = lax.rem(m, 3)
        s_prev = lax.rem(m + 2, 3)
        i_cur = lax.rem(m, 2)
        i_prev = lax.rem(m + 1, 2)
        mt = m_ref[...]
        vstart = key_start(jnp.maximum(m - 1, 0))
        for h in range(nh):
            p = h // 2
            z_prev = z_refs[h][s_prev]
            incl = incl_refs[h][i_prev]
            z_cur = z_refs[h][s_cur]
            car = car_refs[h][...]
            w = jnp.exp2(z_prev - incl - car)
            vtblk = vt_ref[p * width:(p + 1) * width, pl.ds(vstart, blk)]
            acc_refs[h][...] += jnp.dot(vtblk, w.astype(BF16), preferred_element_type=F32)
            car_refs[h][...] = car + incl[0:1, :]
            nabs = pltpu.bitcast(pltpu.bitcast(z_cur, jnp.uint32) | sign, F32)
            sp2 = jnp.maximum(z_cur, 0.0) + jnp.log2(1.0 + jnp.exp2(nabs))
            incl_refs[h][i_cur] = jnp.dot(mt, sp2.astype(BF16), preferred_element_type=F32)
            z_refs[h][s_next] = scores(m + 1, h)
        return c

    lax.fori_loop(0, nblocks + 1, body, 0)
    srow = lax.broadcasted_iota(jnp.int32, (width, 1), 0)
    for p in range(npair):
        both = jnp.where(srow < hd, acc_refs[2 * p][...], acc_refs[2 * p + 1][...])
        o_ref[:, p * width:(p + 1) * width] = both.T.astype(o_ref.dtype)


def _sb_attention(qk, vt, b, s, *, heads, hd):
    t = qk.shape[0]
    blk = _tile(s, ATTN_BLOCK, V7X_LANES)
    nq = s // blk
    width = 2 * hd
    assert width == V7X_LANES and heads % (2 * ATTN_PAIRS) == 0
    npair = ATTN_PAIRS
    gw = npair * width
    ng = heads * hd // gw
    m_t = (jnp.arange(blk)[None, :] >= jnp.arange(blk)[:, None]).astype(BF16)
    blocks = 2 * _nbytes((blk, gw), BF16) + 2 * _nbytes((s, gw), BF16) + _nbytes((blk, blk), BF16)
    nh = 2 * npair
    scratch = nh * (_nbytes((width, blk), F32) + _nbytes((V7X_SUBLANES, blk), F32) + 5 * _nbytes((blk, blk), F32))
    temps = 4 * nh * _nbytes((blk, blk), F32)
    return pl.pallas_call(
        functools.partial(_sb_attn_kernel, blk=blk, hd=hd, npair=npair),
        grid=(b, ng, nq),
        in_specs=[pl.BlockSpec((blk, gw), lambda bi, g, qi: (bi * nq + qi, g)),
                  pl.BlockSpec((s, gw), lambda bi, g, qi: (bi, ng + g)),
                  pl.BlockSpec((gw, s), lambda bi, g, qi: (g, bi)),
                  pl.BlockSpec((blk, blk), lambda bi, g, qi: (0, 0))],
        out_specs=pl.BlockSpec((blk, gw), lambda bi, g, qi: (bi * nq + qi, g)),
        out_shape=jax.ShapeDtypeStruct((t, heads * hd), BF16),
        scratch_shapes=([pltpu.VMEM((width, blk), F32)] * nh + [pltpu.VMEM((1, blk), F32)] * nh
                        + [pltpu.VMEM((3, blk, blk), F32)] * nh + [pltpu.VMEM((2, blk, blk), F32)] * nh),
        compiler_params=_params(("parallel", "parallel", "arbitrary"), _vmem_limit(blocks, scratch, temps)),
        name="sb_attention",
    )(qk, qk, vt, m_t)


def _ssd_kernel(xbc_ref, z_ref, dt_ref, cw_ref, cb_ref, dtb_ref, alog_ref, dskip_ref, ng_ref, tri_ref, exp_ref,
                o_ref, ext_ref, st_ref, *, q, width, nstate, hd, kconv):
    halo = V7X_SUBLANES
    gw = width // SSD_GROUPS

    @pl.when(pl.program_id(1) == 0)
    def _():
        ext_ref[0:halo, :] = jnp.zeros((halo, ext_ref.shape[1]), F32)
        st_ref[...] = jnp.zeros_like(st_ref)

    x_new = xbc_ref[...]
    ext_ref[halo:halo + q, :] = x_new
    conv = cb_ref[...] + cw_ref[0:1, :] * ext_ref[pl.ds(halo - (kconv - 1), q), :]
    for k in range(1, kconv):
        conv = conv + cw_ref[k:k + 1, :] * ext_ref[pl.ds(halo - (kconv - 1) + k, q), :]
    ext_ref[0:halo, :] = x_new[q - halo:q, :]
    xbc = _silu(conv)
    xs = xbc[:, :width]
    bm = xbc[:, width:width + SSD_GROUPS * nstate]
    cm = xbc[:, width + SSD_GROUPS * nstate:]

    dt = _softplus(dt_ref[...] + dtb_ref[...])
    a = -jnp.exp(alog_ref[...])
    acum = _dot_exact_lhs(tri_ref[...], dt * a)
    ea = jnp.exp(acum)
    dte = jnp.exp(acum[q - 1:q, :] - acum)
    ex = exp_ref[...]
    dt_x = _dot_exact_rhs(dt, ex)
    ea_x = _dot_exact_rhs(ea, ex)
    dte_x = _dot_exact_rhs(dte, ex)
    xdt = xs * dt_x
    xdt_b = xdt.astype(BF16)
    xdec_b = (xdt * dte_x).astype(BF16)
    acum_t = acum.T

    row = lax.broadcasted_iota(jnp.int32, (q, q), 0)
    col = lax.broadcasted_iota(jnp.int32, (q, q), 1)
    causal = col <= row
    lane = lax.broadcasted_iota(jnp.int32, (1, 2 * hd), 1)
    heads_per_group = gw // hd
    y_parts = []
    for g in range(SSD_GROUPS):
        bg = bm[:, g * nstate:(g + 1) * nstate].astype(BF16)
        cg = cm[:, g * nstate:(g + 1) * nstate].astype(BF16)
        gmat = lax.dot_general(cg, bg, (((1,), (1,)), ((), ())), preferred_element_type=F32)
        for pr in range(heads_per_group // 2):
            c0 = g * gw + pr * 2 * hd
            xpair = xdt_b[:, c0:c0 + 2 * hd]
            yd = []
            for hh in range(2):
                h = (c0 // hd) + hh
                seg = acum[:, h:h + 1] - acum_t[h:h + 1, :]
                lmat = jnp.exp(jnp.where(causal, seg, -jnp.inf))
                yd.append(jnp.dot((gmat * lmat).astype(BF16), xpair, preferred_element_type=F32))
            y_parts.append(jnp.where(lane < hd, yd[0], yd[1]))
        hprev = st_ref[:, g * gw:(g + 1) * gw]
        y_off = jnp.dot(cg, hprev.astype(BF16), preferred_element_type=F32) * ea_x[:, g * gw:(g + 1) * gw]
        st_new = lax.dot_general(bg, xdec_b[:, g * gw:(g + 1) * gw], (((0,), (0,)), ((), ())),
                                 preferred_element_type=F32)
        st_ref[:, g * gw:(g + 1) * gw] = hprev * ea_x[q - 1:q, g * gw:(g + 1) * gw] + st_new
        y_parts.append(y_off)
    pp = heads_per_group // 2 + 1
    ys = []
    for g in range(SSD_GROUPS):
        diag = jnp.concatenate(y_parts[g * pp:g * pp + pp - 1], axis=1)
        ys.append(diag + y_parts[g * pp + pp - 1])
    y = jnp.concatenate(ys, axis=1) + xs * dskip_ref[...]
    o_ref[...] = _rms(y * _silu(z_ref[...]), ng_ref[...]).astype(o_ref.dtype)


def _ssd(xbc, z, dt, conv_w, conv_b, dt_bias, a_log, d_skip, norm_g, b, s, *, heads, hd, nstate):
    t, xbcw = xbc.shape
    width = heads * hd
    q = _tile(s, SSD_CHUNK, V7X_LANES)
    nc = s // q
    kconv = conv_w.shape[0]
    cw = jnp.zeros((V7X_SUBLANES, xbcw), F32).at[:kconv].set(conv_w)
    pad = lambda v: jnp.zeros((1, V7X_LANES), F32).at[0, :heads].set(v)
    tri = (jnp.arange(q)[None, :] <= jnp.arange(q)[:, None]).astype(BF16)
    hx = (jnp.arange(V7X_LANES)[:, None] == (jnp.arange(width)[None, :] // hd)).astype(BF16)
    row = lambda i, c: (i * nc + c, 0)
    fix = lambda i, c: (0, 0)
    blocks = (_nbytes((q, xbcw + width + V7X_LANES), F32) + _nbytes((q, width), BF16)
              + _nbytes((q, q), BF16) + _nbytes((V7X_LANES, width), BF16) + 8 * _nbytes((1, xbcw), F32))
    scratch = _nbytes((q + V7X_SUBLANES, xbcw), F32) + _nbytes((nstate, width), F32)
    temps = 24 * _nbytes((q, xbcw), F32)
    return pl.pallas_call(
        functools.partial(_ssd_kernel, q=q, width=width, nstate=nstate, hd=hd, kconv=kconv),
        grid=(b, nc),
        in_specs=[pl.BlockSpec((q, xbcw), row), pl.BlockSpec((q, width), row), pl.BlockSpec((q, V7X_LANES), row),
                  pl.BlockSpec((V7X_SUBLANES, xbcw), fix), pl.BlockSpec((1, xbcw), fix),
                  pl.BlockSpec((1, V7X_LANES), fix), pl.BlockSpec((1, V7X_LANES), fix),
                  pl.BlockSpec((1, width), fix), pl.BlockSpec((1, width), fix),
                  pl.BlockSpec((q, q), fix), pl.BlockSpec((V7X_LANES, width), fix)],
        out_specs=pl.BlockSpec((q, width), row),
        out_shape=jax.ShapeDtypeStruct((t, width), BF16),
        scratch_shapes=[pltpu.VMEM((q + V7X_SUBLANES, xbcw), F32), pltpu.VMEM((nstate, width), F32)],
        compiler_params=_params(("parallel", "arbitrary"), _vmem_limit(blocks, scratch, temps)),
        name="ssd",
    )(xbc, z, dt, cw, conv_b.reshape(1, xbcw), pad(dt_bias), pad(a_log),
      jnp.repeat(d_skip, hd).reshape(1, width), norm_g.reshape(1, width), tri, hx)


def _out_proj_kernel(x_ref, a_ref, y_ref, w_ref, o_ref, *, wa):
    acc = jnp.dot(a_ref[...], w_ref[0:wa, :], preferred_element_type=F32)
    acc = acc + jnp.dot(y_ref[...], w_ref[wa:, :], preferred_element_type=F32)
    o_ref[...] = x_ref[...] + acc


def _out_proj(x2, attn, ynorm, w):
    t, d = x2.shape
    wa, wy = attn.shape[1], ynorm.shape[1]
    tm = _tile(t, ROW_TILE)
    row = lambda i: (i, 0)
    blocks = 2 * _nbytes((tm, d), F32) + _nbytes((tm, wa + wy), BF16) + _nbytes(w.shape, BF16)
    return pl.pallas_call(
        functools.partial(_out_proj_kernel, wa=wa),
        grid=(t // tm,),
        in_specs=[pl.BlockSpec((tm, d), row), pl.BlockSpec((tm, wa), row), pl.BlockSpec((tm, wy), row),
                  pl.BlockSpec(w.shape, lambda i: (0, 0))],
        out_specs=pl.BlockSpec((tm, d), row),
        out_shape=jax.ShapeDtypeStruct((t, d), F32),
        compiler_params=_params(("parallel",), _vmem_limit(blocks, 0, 2 * _nbytes((tm, d), F32))),
        name="out_proj",
    )(x2, attn, ynorm, w)


def _ffn_kernel(h_ref, g_ref, w1_ref, w3_ref, w2_ref, o_ref, n_ref, acc_ref):
    f = pl.program_id(1)

    @pl.when(f == 0)
    def _():
        n_ref[...] = _rms(h_ref[...], g_ref[...]).astype(BF16)
        acc_ref[...] = jnp.zeros_like(acc_ref)

    n = n_ref[...]
    a = _silu(jnp.dot(n, w1_ref[...], preferred_element_type=F32)) * jnp.dot(n, w3_ref[...], preferred_element_type=F32)
    acc_ref[...] += jnp.dot(a.astype(BF16), w2_ref[...], preferred_element_type=F32)

    @pl.when(f == pl.num_programs(1) - 1)
    def _():
        o_ref[...] = h_ref[...] + acc_ref[...]


def _ffn(h, g, w1, w3, w2):
    t, d = h.shape
    ff = w1.shape[1]
    tm = _tile(t, FFN_ROW_TILE)
    tf = _tile(ff, FFN_COL_TILE, V7X_LANES)
    blocks = 2 * _nbytes((tm, d), F32) + 3 * _nbytes((d, tf), BF16)
    scratch = _nbytes((tm, d), BF16) + _nbytes((tm, d), F32)
    temps = 4 * _nbytes((tm, tf), F32) + _nbytes((tm, d), F32)
    return pl.pallas_call(
        _ffn_kernel,
        grid=(t // tm, ff // tf),
        in_specs=[pl.BlockSpec((tm, d), lambda i, f: (i, 0)), pl.BlockSpec((1, d), lambda i, f: (0, 0)),
                  pl.BlockSpec((d, tf), lambda i, f: (0, f)), pl.BlockSpec((d, tf), lambda i, f: (0, f)),
                  pl.BlockSpec((tf, d), lambda i, f: (f, 0))],
        out_specs=pl.BlockSpec((tm, d), lambda i, f: (i, 0)),
        out_shape=jax.ShapeDtypeStruct((t, d), F32),
        scratch_shapes=[pltpu.VMEM((tm, d), BF16), pltpu.VMEM((tm, d), F32)],
        compiler_params=_params(("parallel", "arbitrary"), _vmem_limit(blocks, scratch, temps)),
        name="ffn",
    )(h, g, w1, w3, w2)


def _ple_kernel(h_ref, p_ref, g_ref, wg_ref, wp_ref, o_ref):
    h = h_ref[...]
    n = _rms(h, g_ref[...]).astype(BF16)
    gate = _sigmoid(jnp.dot(n, wg_ref[...], preferred_element_type=F32))
    emb = jnp.dot(p_ref[...].astype(BF16), wp_ref[...], preferred_element_type=F32)
    o_ref[...] = h + emb * gate


def _ple(h, p2, g, w_gate, w_ple):
    t, d = h.shape
    pd = p2.shape[1]
    tm = _tile(t, ROW_TILE)
    row = lambda i: (i, 0)
    fix = lambda i: (0, 0)
    blocks = 2 * _nbytes((tm, d), F32) + _nbytes((tm, pd), F32) + _nbytes((d + pd, d), BF16)
    return pl.pallas_call(
        _ple_kernel,
        grid=(t // tm,),
        in_specs=[pl.BlockSpec((tm, d), row), pl.BlockSpec((tm, pd), row), pl.BlockSpec((1, d), fix),
                  pl.BlockSpec((d, d), fix), pl.BlockSpec((pd, d), fix)],
        out_specs=pl.BlockSpec((tm, d), row),
        out_shape=jax.ShapeDtypeStruct((t, d), F32),
        compiler_params=_params(("parallel",), _vmem_limit(blocks, 0, 4 * _nbytes((tm, d), F32))),
        name="ple",
    )(h, p2, g, w_gate, w_ple)


def _glu_kernel(h_ref, g_ref, w_ref, b_ref, o_ref, *, c):
    n = _rms(h_ref[...], g_ref[...]).astype(BF16)
    val = jnp.dot(n, w_ref[:, 0:c], preferred_element_type=F32) + b_ref[:, 0:c]
    gate = jnp.dot(n, w_ref[:, c:], preferred_element_type=F32) + b_ref[:, c:]
    o_ref[...] = val * _sigmoid(gate)


def _glu(h, g, w, bias):
    t, d = h.shape
    c = w.shape[1] // 2
    tm = _tile(t, ROW_TILE)
    row = lambda i: (i, 0)
    fix = lambda i: (0, 0)
    blocks = _nbytes((tm, d), F32) + _nbytes((tm, c), F32) + _nbytes(w.shape, BF16)
    return pl.pallas_call(
        functools.partial(_glu_kernel, c=c),
        grid=(t // tm,),
        in_specs=[pl.BlockSpec((tm, d), row), pl.BlockSpec((1, d), fix), pl.BlockSpec(w.shape, fix),
                  pl.BlockSpec((1, 2 * c), fix)],
        out_specs=pl.BlockSpec((tm, c), row),
        out_shape=jax.ShapeDtypeStruct((t, c), F32),
        compiler_params=_params(("parallel",), _vmem_limit(blocks, 0, 4 * _nbytes((tm, c), F32))),
        name="glu",
    )(h, g, w, bias)


def _dwconv_kernel(u_ref, h_ref, dw_ref, db_ref, lg_ref, lb_ref, w_ref, b_ref, o_ref, ext_ref, sh_ref, cv_ref,
                   *, ts, kconv, rc):
    halo = CONV_HALO
    sub = V7X_SUBLANES

    @pl.when(pl.program_id(1) == 0)
    def _():
        ext_ref[0:halo, :] = jnp.zeros((halo, ext_ref.shape[1]), F32)

    ext_ref[halo:halo + ts, :] = u_ref[...]
    base = halo - (kconv - 1)
    for r in range(1, sub):
        sh_ref[r - 1] = ext_ref[pl.ds(r, sh_ref.shape[1]), :]

    def tap(k, r0):
        a, r = divmod(base + k, sub)
        start = pl.multiple_of(r0 + a * sub, sub)
        if r == 0:
            return ext_ref[pl.ds(start, rc), :]
        return sh_ref[r - 1, pl.ds(start, rc), :]

    def chunk(i, c):
        r0 = pl.multiple_of(i * rc, rc)
        acc = db_ref[...] + dw_ref[0:1, :] * tap(0, r0)
        for k in range(1, kconv):
            acc = acc + dw_ref[k:k + 1, :] * tap(k, r0)
        cv_ref[pl.ds(r0, rc), :] = acc
        return c

    lax.fori_loop(0, ts // rc, chunk, 0)
    ext_ref[0:halo, :] = ext_ref[ts:ts + halo, :]

    cv = cv_ref[...]
    mu = jnp.mean(cv, axis=-1, keepdims=True)
    xc = cv - mu
    var = jnp.mean(xc * xc, axis=-1, keepdims=True)
    ln = xc * lax.rsqrt(var + EPS) * lg_ref[...] + lb_ref[...]
    act = _silu(ln).astype(BF16)
    o_ref[...] = h_ref[...] + jnp.dot(act, w_ref[...], preferred_element_type=F32) + b_ref[...]


def _dwconv(u, h, dw_w, dw_b, ln_g, ln_b, w2, b2, b, s):
    t, c = u.shape
    d = h.shape[1]
    kconv = dw_w.shape[0]
    assert kconv - 1 <= CONV_HALO
    ts = _tile(s, CONV_ROW_TILE)
    rc = _tile(ts, CONV_ROW_CHUNK)
    nt = s // ts
    kp = -(-kconv // V7X_SUBLANES) * V7X_SUBLANES
    dwp = jnp.zeros((kp, c), F32).at[:kconv].set(dw_w)
    row = lambda i, j: (i * nt + j, 0)
    fix = lambda i, j: (0, 0)
    blocks = _nbytes((ts, c), F32) + 2 * _nbytes((ts, d), F32) + _nbytes((c, d), BF16) + _nbytes((kp, c), F32)
    sh_rows = ts + CONV_HALO - V7X_SUBLANES
    scratch = _nbytes((ts + CONV_HALO, c), F32) + _nbytes((ts, c), F32) + _nbytes((V7X_SUBLANES - 1, sh_rows, c), F32)
    return pl.pallas_call(
        functools.partial(_dwconv_kernel, ts=ts, kconv=kconv, rc=rc),
        grid=(b, nt),
        in_specs=[pl.BlockSpec((ts, c), row), pl.BlockSpec((ts, d), row), pl.BlockSpec((kp, c), fix),
                  pl.BlockSpec((1, c), fix), pl.BlockSpec((1, c), fix), pl.BlockSpec((1, c), fix),
                  pl.BlockSpec((c, d), fix), pl.BlockSpec((1, d), fix)],
        out_specs=pl.BlockSpec((ts, d), row),
        out_shape=jax.ShapeDtypeStruct((t, d), F32),
        scratch_shapes=[pltpu.VMEM((ts + CONV_HALO, c), F32), pltpu.VMEM((V7X_SUBLANES - 1, sh_rows, c), F32),
                        pltpu.VMEM((ts, c), F32)],
        compiler_params=_params(("parallel", "arbitrary"), _vmem_limit(blocks, scratch, 6 * _nbytes((ts, c), F32))),
        name="dwconv",
    )(u, h, dwp, dw_b.reshape(1, c), ln_g.reshape(1, c), ln_b.reshape(1, c), w2, b2.reshape(1, d))


def _router_kernel(h_ref, g_ref, wr_ref, n_ref, meta_ref, *, ne):
    nf = _rms(h_ref[...], g_ref[...])
    n_ref[...] = nf
    logits = jnp.dot(nf.astype(BF16), wr_ref[...], preferred_element_type=F32)
    lane = lax.broadcasted_iota(jnp.int32, logits.shape, 1)
    neg = jnp.float32(-jnp.inf)
    logits = jnp.where(lane < ne, logits, neg)
    big = jnp.int32(V7X_LANES)
    m1 = jnp.max(logits, axis=-1, keepdims=True)
    i1 = jnp.min(jnp.where(logits == m1, lane, big), axis=-1, keepdims=True)
    sel1 = lane == i1
    rest = jnp.where(sel1, neg, logits)
    m2 = jnp.max(rest, axis=-1, keepdims=True)
    i2 = jnp.min(jnp.where(rest == m2, lane, big), axis=-1, keepdims=True)
    e2 = jnp.exp(m2 - m1)
    g1 = 1.0 / (1.0 + e2)
    g2 = e2 / (1.0 + e2)
    meta = jnp.where(lane == 0, g1, jnp.where(lane == 1, g2, 0.0))
    meta = jnp.where(lane == 2, i1.astype(F32), jnp.where(lane == 3, i2.astype(F32), meta))
    meta_ref[...] = meta


def _router(h, g, wr_pad, ne):
    t, d = h.shape
    tm = _tile(t, ROW_TILE)
    row = lambda i: (i, 0)
    fix = lambda i: (0, 0)
    blocks = 2 * _nbytes((tm, d), F32) + _nbytes((tm, V7X_LANES), F32) + _nbytes(wr_pad.shape, BF16)
    return pl.pallas_call(
        functools.partial(_router_kernel, ne=ne),
        grid=(t // tm,),
        in_specs=[pl.BlockSpec((tm, d), row), pl.BlockSpec((1, d), fix), pl.BlockSpec(wr_pad.shape, fix)],
        out_specs=[pl.BlockSpec((tm, d), row), pl.BlockSpec((tm, V7X_LANES), row)],
        out_shape=[jax.ShapeDtypeStruct((t, d), F32), jax.ShapeDtypeStruct((t, V7X_LANES), F32)],
        compiler_params=_params(("parallel",), _vmem_limit(blocks, 0, 2 * _nbytes((tm, d), F32))),
        name="router",
    )(h, g, wr_pad)


def _moe_plan(meta, tm, ne):
    t = meta.shape[0]
    na = TOP_K * t
    nt = na // tm + ne
    e_flat = meta[:, 2:2 + TOP_K].astype(jnp.int32).reshape(na)
    onehot = (e_flat[:, None] == jnp.arange(ne, dtype=jnp.int32)[None, :]).astype(jnp.int32)
    csum = jnp.cumsum(onehot, axis=0)
    counts = csum[-1]
    rank = jnp.sum(csum * onehot, axis=1) - 1
    padded = ((counts + tm - 1) // tm) * tm
    gend = jnp.cumsum(padded)
    gstart = gend - padded
    slot = (gstart[e_flat] + rank).reshape(t, TOP_K)
    tile_start = jnp.arange(nt, dtype=jnp.int32) * tm
    tile_e = jnp.sum((tile_start[:, None] >= gend[None, :]).astype(jnp.int32), axis=1)
    valid = (tile_e < ne).astype(jnp.int32)
    return slot, jnp.minimum(tile_e, ne - 1), valid, nt * tm


def _sc_geometry(t, d, dtype):
    sc = pltpu.get_tpu_info().sparse_core
    nc, nw = sc.num_cores, sc.num_cores * sc.num_subcores
    per_w = t // nw
    chunk = _tile(per_w, SC_GATHER_ROWS)
    assert t % nw == 0 and per_w % chunk == 0 and chunk % V7X_SUBLANES == 0
    assert TOP_K * chunk * (d * jnp.dtype(dtype).itemsize + 4) <= sc.vmem_capacity_bytes
    mesh = plsc.VectorSubcoreMesh(core_axis_name="c", subcore_axis_name="s")
    return nc, per_w, chunk, mesh


def _dispatch_rows(rows, slot0, slot1, n_slots):
    t, d = rows.shape
    nc, per_w, chunk, mesh = _sc_geometry(t, d, rows.dtype)

    @functools.partial(
        pl.kernel, mesh=mesh, out_type=jax.ShapeDtypeStruct((n_slots, d), rows.dtype),
        scratch_types=[pltpu.VMEM((chunk,), jnp.int32), pltpu.VMEM((chunk,), jnp.int32),
                       pltpu.VMEM((chunk, d), rows.dtype)])
    def dispatch(rows_hbm, s0_hbm, s1_hbm, out_hbm, i0_v, i1_v, rows_v):
        base = (lax.axis_index("s") * nc + lax.axis_index("c")) * per_w

        @pl.loop(0, per_w // chunk)
        def _(c):
            off = pl.multiple_of(base + c * chunk, chunk)
            pltpu.sync_copy(s0_hbm.at[pl.ds(off, chunk)], i0_v)
            pltpu.sync_copy(s1_hbm.at[pl.ds(off, chunk)], i1_v)
            pltpu.sync_copy(rows_hbm.at[pl.ds(off, chunk)], rows_v)
            pltpu.sync_copy(rows_v, out_hbm.at[i0_v])
            pltpu.sync_copy(rows_v, out_hbm.at[i1_v])

    return dispatch(rows, slot0, slot1)


def _combine_rows(table, slot0, slot1):
    t = slot0.shape[0]
    d = table.shape[1]
    nc, per_w, chunk, mesh = _sc_geometry(t, d, table.dtype)
    out = jax.ShapeDtypeStruct((t, d), table.dtype)

    @functools.partial(
        pl.kernel, mesh=mesh, out_type=(out, out),
        scratch_types=[pltpu.VMEM((chunk,), jnp.int32), pltpu.VMEM((chunk,), jnp.int32),
                       pltpu.VMEM((chunk, d), table.dtype), pltpu.VMEM((chunk, d), table.dtype),
                       pltpu.SemaphoreType.DMA, pltpu.SemaphoreType.DMA])
    def combine(table_hbm, s0_hbm, s1_hbm, y0_hbm, y1_hbm, i0_v, i1_v, r0_v, r1_v, sem0, sem1):
        base = (lax.axis_index("s") * nc + lax.axis_index("c")) * per_w

        @pl.loop(0, per_w // chunk)
        def _(c):
            off = pl.multiple_of(base + c * chunk, chunk)
            pltpu.sync_copy(s0_hbm.at[pl.ds(off, chunk)], i0_v)
            pltpu.sync_copy(s1_hbm.at[pl.ds(off, chunk)], i1_v)
            g0 = pltpu.async_copy(table_hbm.at[i0_v], r0_v, sem0)
            g1 = pltpu.async_copy(table_hbm.at[i1_v], r1_v, sem1)
            g0.wait()
            pltpu.sync_copy(r0_v, y0_hbm.at[pl.ds(off, chunk)])
            g1.wait()
            pltpu.sync_copy(r1_v, y1_hbm.at[pl.ds(off, chunk)])

    return combine(table, slot0, slot1)


def _moe_kernel(te_ref, tv_ref, x_ref, w1_ref, w3_ref, w2_ref, o_ref, n_ref, acc_ref):
    i = pl.program_id(0)
    f = pl.program_id(1)
    last = pl.num_programs(1) - 1
    valid = tv_ref[i] == 1

    @pl.when(valid & (f == 0))
    def _():
        n_ref[...] = x_ref[...].astype(BF16)
        acc_ref[...] = jnp.zeros_like(acc_ref)

    @pl.when(valid)
    def _():
        n = n_ref[...]
        w1 = w1_ref[0].astype(BF16)
        w3 = w3_ref[0].astype(BF16)
        w2 = w2_ref[0].astype(BF16)
        a = _silu(jnp.dot(n, w1, preferred_element_type=F32)) * jnp.dot(n, w3, preferred_element_type=F32)
        acc_ref[...] += jnp.dot(a.astype(BF16), w2, preferred_element_type=F32)

    @pl.when(valid & (f == last))
    def _():
        o_ref[...] = acc_ref[...]

    @pl.when(jnp.logical_not(valid) & (f == last))
    def _():
        o_ref[...] = jnp.zeros_like(o_ref)


def _moe_experts(xs, tile_e, tile_valid, w1, w3, w2, tm):
    ns, d = xs.shape
    ne, _, ff = w1.shape
    tf = _tile(ff, MOE_COL_TILE, V7X_LANES)
    nf = ff // tf
    fidx = lambda f, tv, i: f * tv[i] + (nf - 1) * (1 - tv[i])
    blocks = 2 * _nbytes((tm, d), F32) + 3 * _nbytes((d, tf), w1.dtype)
    scratch = _nbytes((tm, d), BF16) + _nbytes((tm, d), F32)
    temps = 4 * _nbytes((tm, tf), F32) + _nbytes((tm, d), F32) + 3 * _nbytes((d, tf), BF16)
    grid_spec = pltpu.PrefetchScalarGridSpec(
        num_scalar_prefetch=2,
        grid=(ns // tm, nf),
        in_specs=[pl.BlockSpec((tm, d), lambda i, f, te, tv: (i, 0)),
                  pl.BlockSpec((1, d, tf), lambda i, f, te, tv: (te[i], 0, fidx(f, tv, i))),
                  pl.BlockSpec((1, d, tf), lambda i, f, te, tv: (te[i], 0, fidx(f, tv, i))),
                  pl.BlockSpec((1, tf, d), lambda i, f, te, tv: (te[i], fidx(f, tv, i), 0))],
        out_specs=pl.BlockSpec((tm, d), lambda i, f, te, tv: (i, 0)),
        scratch_shapes=[pltpu.VMEM((tm, d), BF16), pltpu.VMEM((tm, d), F32)])
    return pl.pallas_call(
        _moe_kernel,
        grid_spec=grid_spec,
        out_shape=jax.ShapeDtypeStruct((ns, d), F32),
        compiler_params=_params(("arbitrary", "arbitrary"), _vmem_limit(blocks, scratch, temps)),
        name="moe_experts",
    )(tile_e, tile_valid, xs, w1, w3, w2)


def _combine_ple_kernel(h_ref, y0_ref, y1_ref, m_ref, p_ref, g_ref, wg_ref, wp_ref, fg_ref, o_ref):
    meta = m_ref[...]
    h = h_ref[...] + meta[:, 0:1] * y0_ref[...] + meta[:, 1:2] * y1_ref[...]
    n = _rms(h, g_ref[...]).astype(BF16)
    gate = _sigmoid(jnp.dot(n, wg_ref[...], preferred_element_type=F32))
    emb = jnp.dot(p_ref[...].astype(BF16), wp_ref[...], preferred_element_type=F32)
    o_ref[...] = _rms(h + emb * gate, fg_ref[...])


def _combine_ple(h, y0, y1, meta, p2, g, w_gate, w_ple, final_g):
    t, d = h.shape
    pd = p2.shape[1]
    tm = _tile(t, ROW_TILE)
    row = lambda i: (i, 0)
    fix = lambda i: (0, 0)
    blocks = (4 * _nbytes((tm, d), F32) + _nbytes((tm, pd + V7X_LANES), F32) + _nbytes((d + pd, d), BF16))
    return pl.pallas_call(
        _combine_ple_kernel,
        grid=(t // tm,),
        in_specs=[pl.BlockSpec((tm, d), row), pl.BlockSpec((tm, d), row), pl.BlockSpec((tm, d), row),
                  pl.BlockSpec((tm, V7X_LANES), row),
                  pl.BlockSpec((tm, pd), row), pl.BlockSpec((1, d), fix),
                  pl.BlockSpec((d, d), fix), pl.BlockSpec((pd, d), fix), pl.BlockSpec((1, d), fix)],
        out_specs=pl.BlockSpec((tm, d), row),
        out_shape=jax.ShapeDtypeStruct((t, d), F32),
        compiler_params=_params(("parallel",), _vmem_limit(blocks, 0, 4 * _nbytes((tm, d), F32))),
        name="combine_ple_final",
    )(h, y0, y1, meta, p2, g, w_gate, w_ple, final_g)


def kernel(x, p, mix_norm_g, ffn_norm_g, ple_norm_g, w_ple, w_ple_gate, ev_w_in, ev_conv_w, ev_conv_b, ev_dt_bias, ev_a_log, ev_d, ev_ssd_norm_g, ev_w_out, ev_ffn_w1, ev_ffn_w3, ev_ffn_w2, od_pw1_w, od_pw1_b, od_dw_w, od_dw_b, od_ln_g, od_ln_b, od_pw2_w, od_pw2_b, od_router_w, od_moe_w1, od_moe_w3, od_moe_w2, final_norm_g):
    b, s, d = x.shape
    t = b * s
    depth = p.shape[0]
    assert depth == 2 and ev_w_in.shape[0] == 1 and od_pw1_w.shape[0] == 1
    sbw = SB_HEADS * SB_HEAD_DIM
    ssdw = SSD_HEADS * SSD_HEAD_DIM
    xbcw = ssdw + 2 * SSD_GROUPS * SSD_STATE
    in_width = 3 * sbw + ssdw + xbcw + SSD_HEADS
    assert ev_w_in.shape[2] == in_width
    bf = lambda w: w.astype(BF16)
    vec = lambda v: v.reshape(1, -1)

    x2 = x.reshape(t, d)
    p2 = p.reshape(depth, t, -1)

    w_all = bf(ev_w_in[0])
    w_rest = w_all[:, 3 * sbw:]
    w_in = jnp.concatenate([w_all[:, :2 * sbw], w_rest,
                            jnp.zeros((d, V7X_LANES - SSD_HEADS), BF16)], axis=1)
    w_vt = w_all[:, 2 * sbw:3 * sbw].T
    qscale = LOG2E / math.sqrt(SB_HEAD_DIM)
    qk, vt, z, xbc, dt = _in_proj(x2, vec(mix_norm_g[0]), w_in, w_vt, sbw=sbw, zw=ssdw, xbcw=xbcw, qscale=qscale)
    attn = _sb_attention(qk, vt, b, s, heads=SB_HEADS, hd=SB_HEAD_DIM)
    ynorm = _ssd(xbc, z, dt, ev_conv_w[0], ev_conv_b[0], ev_dt_bias[0], ev_a_log[0], ev_d[0], ev_ssd_norm_g[0],
                 b, s, heads=SSD_HEADS, hd=SSD_HEAD_DIM, nstate=SSD_STATE)
    h = _out_proj(x2, attn, ynorm, bf(ev_w_out[0]))
    h = _ffn(h, vec(ffn_norm_g[0]), bf(ev_ffn_w1[0]), bf(ev_ffn_w3[0]), bf(ev_ffn_w2[0]))
    h = _ple(h, p2[0], vec(ple_norm_g[0]), bf(w_ple_gate[0]), bf(w_ple[0]))

    u = _glu(h, vec(mix_norm_g[1]), bf(od_pw1_w[0]), vec(od_pw1_b[0]))
    h = _dwconv(u, h, od_dw_w[0], od_dw_b[0], od_ln_g[0], od_ln_b[0], bf(od_pw2_w[0]), od_pw2_b[0], b, s)
    ne = od_router_w.shape[2]
    wr = jnp.zeros((d, V7X_LANES), BF16).at[:, :ne].set(bf(od_router_w[0]))
    n, meta = _router(h, vec(ffn_norm_g[1]), wr, ne)
    tm = _tile(TOP_K * t, FFN_ROW_TILE)
    slot, tile_e, tile_valid, n_slots = _moe_plan(meta, tm, ne)
    xs = _dispatch_rows(n, slot[:, 0], slot[:, 1], n_slots)
    ys = _moe_experts(xs, tile_e, tile_valid, od_moe_w1[0], od_moe_w3[0], od_moe_w2[0], tm)
    y0, y1 = _combine_rows(ys, slot[:, 0], slot[:, 1])
    out = _combine_ple(h, y0, y1, meta, p2[1], vec(ple_norm_g[1]), bf(w_ple_gate[1]), bf(w_ple[1]), vec(final_norm_g))
    return out.reshape(b, s, d)
```

```python
import functools
import math

import jax
import jax.numpy as jnp
from jax import lax
from jax.experimental import pallas as pl
from jax.experimental.pallas import tpu as pltpu
from jax.experimental.pallas import tpu_sc as plsc

F32 = jnp.float32
BF16 = jnp.bfloat16
EPS = 1e-6
LOG2E = 1.4426950408889634

V7X_LANES = 128
V7X_SUBLANES = 8
V7X_VMEM_BYTES = 64 * 1024 * 1024

SB_HEADS = 8
SB_HEAD_DIM = 64
SSD_HEADS = 8
SSD_HEAD_DIM = 64
SSD_STATE = 128
SSD_GROUPS = 2
SSD_CONV = 4
N_EXPERTS = 8
TOP_K = 2

ROW_TILE = 512
FFN_ROW_TILE = 1024
FFN_COL_TILE = 1408
MOE_COL_TILE = 512
ATTN_BLOCK = 256
ATTN_PAIRS = 2
SSD_CHUNK = 128
CONV_ROW_TILE = 256
CONV_ROW_CHUNK = 32
CONV_HALO = 32
SC_GATHER_ROWS = 32


def _tile(n, pref, mult=V7X_SUBLANES):
    if n <= pref:
        return n
    t = (pref // mult) * mult
    while t > mult and n % t:
        t -= mult
    assert n % t == 0, (n, pref, mult)
    return t


def _vmem_limit(block_bytes, scratch_bytes=0, temp_bytes=0):
    need = 2 * block_bytes + scratch_bytes + temp_bytes + (4 << 20)
    return int(min(max(need, 16 << 20), V7X_VMEM_BYTES - (6 << 20)))


def _params(sem, vmem):
    return pltpu.CompilerParams(dimension_semantics=sem, vmem_limit_bytes=vmem)


def _nbytes(shape, dtype):
    return math.prod(shape) * jnp.dtype(dtype).itemsize


def _rms(x, g):
    return x * lax.rsqrt(jnp.mean(x * x, axis=-1, keepdims=True) + EPS) * g


def _sigmoid(x):
    return 1.0 / (1.0 + jnp.exp(-x))


def _silu(x):
    return x * _sigmoid(x)


def _softplus(x):
    return jnp.maximum(x, 0.0) + jnp.log(1.0 + jnp.exp(-jnp.abs(x)))


def _split3(x):
    hi = x.astype(BF16)
    r1 = x - hi.astype(F32)
    mid = r1.astype(BF16)
    lo = (r1 - mid.astype(F32)).astype(BF16)
    return hi, mid, lo


def _dot_exact_rhs(x, m):
    hi, mid, lo = _split3(x)
    d = functools.partial(jnp.dot, preferred_element_type=F32)
    return d(hi, m) + d(mid, m) + d(lo, m)


def _dot_exact_lhs(m, x):
    hi, mid, lo = _split3(x)
    d = functools.partial(jnp.dot, preferred_element_type=F32)
    return d(m, hi) + d(m, mid) + d(m, lo)


def _in_proj_kernel(x_ref, g_ref, w_ref, wvt_ref, qk_ref, vt_ref, z_ref, xbc_ref, dt_ref, *, sbw, zw, xbcw, qscale):
    hn = _rms(x_ref[...], g_ref[...]).astype(BF16)

    def mm(lo, hi):
        return jnp.dot(hn, w_ref[:, lo:hi], preferred_element_type=F32)

    qk_ref[:, 0:sbw] = (mm(0, sbw) * qscale).astype(BF16)
    qk_ref[:, sbw:2 * sbw] = mm(sbw, 2 * sbw).astype(BF16)
    vt_ref[...] = lax.dot_general(wvt_ref[...], hn, (((1,), (1,)), ((), ())),
                                  preferred_element_type=F32).astype(BF16)
    o = 2 * sbw
    z_ref[...] = mm(o, o + zw)
    o += zw
    xbc_ref[...] = mm(o, o + xbcw)
    o += xbcw
    dt_ref[...] = mm(o, o + V7X_LANES)


def _in_proj(x2, g, w_pad, w_vt, *, sbw, zw, xbcw, qscale):
    t, d = x2.shape
    n = w_pad.shape[1]
    tm = _tile(t, ROW_TILE, V7X_LANES)
    blocks = (_nbytes((tm, d), F32) + _nbytes((d, n + sbw), BF16) + _nbytes((tm, 3 * sbw), BF16)
              + _nbytes((tm, zw + xbcw + V7X_LANES), F32))
    return pl.pallas_call(
        functools.partial(_in_proj_kernel, sbw=sbw, zw=zw, xbcw=xbcw, qscale=qscale),
        grid=(t // tm,),
        in_specs=[pl.BlockSpec((tm, d), lambda i: (i, 0)),
                  pl.BlockSpec((1, d), lambda i: (0, 0)),
                  pl.BlockSpec((d, n), lambda i: (0, 0)),
                  pl.BlockSpec((sbw, d), lambda i: (0, 0))],
        out_specs=[pl.BlockSpec((tm, 2 * sbw), lambda i: (i, 0)),
                   pl.BlockSpec((sbw, tm), lambda i: (0, i)),
                   pl.BlockSpec((tm, zw), lambda i: (i, 0)),
                   pl.BlockSpec((tm, xbcw), lambda i: (i, 0)),
                   pl.BlockSpec((tm, V7X_LANES), lambda i: (i, 0))],
        out_shape=[jax.ShapeDtypeStruct((t, 2 * sbw), BF16),
                   jax.ShapeDtypeStruct((sbw, t), BF16),
                   jax.ShapeDtypeStruct((t, zw), F32),
                   jax.ShapeDtypeStruct((t, xbcw), F32),
                   jax.ShapeDtypeStruct((t, V7X_LANES), F32)],
        compiler_params=_params(("parallel",), _vmem_limit(blocks, 0, _nbytes((tm, n), F32))),
        name="in_proj",
    )(x2, g, w_pad, w_vt)


ATTN_MASKED = -1e30


def _sb_attn_kernel(q_ref, k_ref, vt_ref, m_ref, o_ref, *scratch, blk, hd, npair):
    qi = pl.program_id(2)
    nblocks = qi + 1
    width = 2 * hd
    nh = 2 * npair
    acc_refs, car_refs, z_refs, incl_refs = (scratch[i * nh:(i + 1) * nh] for i in range(4))
    lane = lax.broadcasted_iota(jnp.int32, (1, width), 1)
    qh = []
    for p in range(npair):
        q = q_ref[:, p * width:(p + 1) * width]
        qz = jnp.zeros_like(q)
        qh += [jnp.where(lane < hd, q, qz), jnp.where(lane >= hd, q, qz)]
    sign = jnp.uint32(0x80000000)

    def key_start(m):
        j = jnp.where(m < nblocks, qi - m, qi)
        return pl.multiple_of(j * blk, blk)

    def scores(m, h):
        p = h // 2
        kblk = k_ref[pl.ds(key_start(m), blk), p * width:(p + 1) * width]
        return lax.dot_general(kblk, qh[h], (((1,), (1,)), ((), ())), preferred_element_type=F32)

    def cumsum_stage(h, z_cur, i_cur):
        nabs = pltpu.bitcast(pltpu.bitcast(z_cur, jnp.uint32) | sign, F32)
        sp2 = jnp.maximum(z_cur, 0.0) + jnp.log2(1.0 + jnp.exp2(nabs))
        incl_refs[h][i_cur] = jnp.dot(m_ref[...], sp2.astype(BF16), preferred_element_type=F32)

    def weight_stage(h, z_prev, incl, vstart):
        p = h // 2
        car = car_refs[h][...]
        w = jnp.exp2(z_prev - incl - car)
        vtblk = vt_ref[p * width:(p + 1) * width, pl.ds(vstart, blk)]
        acc_refs[h][...] += jnp.dot(vtblk, w.astype(BF16), preferred_element_type=F32)
        car_refs[h][...] = car + incl[0:1, :]

    row = lax.broadcasted_iota(jnp.int32, (blk, blk), 0)
    col = lax.broadcasted_iota(jnp.int32, (blk, blk), 1)
    for h in range(nh):
        acc_refs[h][...] = jnp.zeros_like(acc_refs[h])
        car_refs[h][...] = jnp.zeros_like(car_refs[h])
        z0 = jnp.where(row < col, scores(0, h), ATTN_MASKED)
        z_refs[h][0] = z0
        cumsum_stage(h, z0, 0)
        z_refs[h][1] = scores(1, h)

    def body(m, c):
        s_next = lax.rem(m + 1, 3)
        s_cur = lax.rem(m, 3)
        s_prev = lax.rem(m + 2, 3)
        i_cur = lax.rem(m, 2)
        i_prev = lax.rem(m + 1, 2)
        vstart = key_start(m - 1)
        for h in range(nh):
            z_prev = z_refs[h][s_prev]
            incl = incl_refs[h][i_prev]
            z_cur = z_refs[h][s_cur]
            weight_stage(h, z_prev, incl, vstart)
            cumsum_stage(h, z_cur, i_cur)
            z_refs[h][s_next] = scores(m + 1, h)
        return c

    lax.fori_loop(1, nblocks, body, 0)
    last = nblocks - 1
    for h in range(nh):
        weight_stage(h, z_refs[h][lax.rem(last, 3)], incl_refs[h][lax.rem(last, 2)], key_start(last))
    srow = lax.broadcasted_iota(jnp.int32, (width, 1), 0)
    for p in range(npair):
        both = jnp.where(srow < hd, acc_refs[2 * p][...], acc_refs[2 * p + 1][...])
        o_ref[:, p * width:(p + 1) * width] = both.T.astype(o_ref.dtype)


def _sb_attention(qk, vt, b, s, *, heads, hd):
    t = qk.shape[0]
    blk = _tile(s, ATTN_BLOCK, V7X_LANES)
    nq = s // blk
    width = 2 * hd
    assert width == V7X_LANES and heads % (2 * ATTN_PAIRS) == 0
    npair = ATTN_PAIRS
    gw = npair * width
    ng = heads * hd // gw
    m_t = (jnp.arange(blk)[None, :] >= jnp.arange(blk)[:, None]).astype(BF16)
    blocks = 2 * _nbytes((blk, gw), BF16) + 2 * _nbytes((s, gw), BF16) + _nbytes((blk, blk), BF16)
    nh = 2 * npair
    scratch = nh * (_nbytes((width, blk), F32) + _nbytes((V7X_SUBLANES, blk), F32) + 5 * _nbytes((blk, blk), F32))
    temps = 4 * nh * _nbytes((blk, blk), F32)
    return pl.pallas_call(
        functools.partial(_sb_attn_kernel, blk=blk, hd=hd, npair=npair),
        grid=(b, ng, nq),
        in_specs=[pl.BlockSpec((blk, gw), lambda bi, g, qi: (bi * nq + qi, g)),
                  pl.BlockSpec((s, gw), lambda bi, g, qi: (bi, ng + g)),
                  pl.BlockSpec((gw, s), lambda bi, g, qi: (g, bi)),
                  pl.BlockSpec((blk, blk), lambda bi, g, qi: (0, 0))],
        out_specs=pl.BlockSpec((blk, gw), lambda bi, g, qi: (bi * nq + qi, g)),
        out_shape=jax.ShapeDtypeStruct((t, heads * hd), BF16),
        scratch_shapes=([pltpu.VMEM((width, blk), F32)] * nh + [pltpu.VMEM((1, blk), F32)] * nh
                        + [pltpu.VMEM((3, blk, blk), F32)] * nh + [pltpu.VMEM((2, blk, blk), F32)] * nh),
        compiler_params=_params(("parallel", "parallel", "arbitrary"), _vmem_limit(blocks, scratch, temps)),
        name="sb_attention",
    )(qk, qk, vt, m_t)


def _ssd_kernel(xbc_ref, z_ref, dt_ref, cw_ref, cb_ref, dtb_ref, alog_ref, dskip_ref, ng_ref, tri_ref, exp_ref,
                o_ref, ext_ref, st_ref, *, q, width, nstate, hd, kconv):
    halo = V7X_SUBLANES
    gw = width // SSD_GROUPS

    @pl.when(pl.program_id(1) == 0)
    def _():
        ext_ref[0:halo, :] = jnp.zeros((halo, ext_ref.shape[1]), F32)
        st_ref[...] = jnp.zeros_like(st_ref)

    x_new = xbc_ref[...]
    ext_ref[halo:halo + q, :] = x_new
    conv = cb_ref[...] + cw_ref[0:1, :] * ext_ref[pl.ds(halo - (kconv - 1), q), :]
    for k in range(1, kconv):
        conv = conv + cw_ref[k:k + 1, :] * ext_ref[pl.ds(halo - (kconv - 1) + k, q), :]
    ext_ref[0:halo, :] = x_new[q - halo:q, :]
    xbc = _silu(conv)
    xs = xbc[:, :width]
    bm = xbc[:, width:width + SSD_GROUPS * nstate]
    cm = xbc[:, width + SSD_GROUPS * nstate:]

    dt = _softplus(dt_ref[...] + dtb_ref[...])
    a = -jnp.exp(alog_ref[...])
    acum = _dot_exact_lhs(tri_ref[...], dt * a)
    ea = jnp.exp(acum)
    dte = jnp.exp(acum[q - 1:q, :] - acum)
    ex = exp_ref[...]
    dt_x = _dot_exact_rhs(dt, ex)
    ea_x = _dot_exact_rhs(ea, ex)
    dte_x = _dot_exact_rhs(dte, ex)
    xdt = xs * dt_x
    xdt_b = xdt.astype(BF16)
    xdec_b = (xdt * dte_x).astype(BF16)
    acum_t = acum.T

    row = lax.broadcasted_iota(jnp.int32, (q, q), 0)
    col = lax.broadcasted_iota(jnp.int32, (q, q), 1)
    causal = col <= row
    lane = lax.broadcasted_iota(jnp.int32, (1, 2 * hd), 1)
    heads_per_group = gw // hd
    y_parts = []
    for g in range(SSD_GROUPS):
        bg = bm[:, g * nstate:(g + 1) * nstate].astype(BF16)
        cg = cm[:, g * nstate:(g + 1) * nstate].astype(BF16)
        gmat = lax.dot_general(cg, bg, (((1,), (1,)), ((), ())), preferred_element_type=F32)
        for pr in range(heads_per_group // 2):
            c0 = g * gw + pr * 2 * hd
            xpair = xdt_b[:, c0:c0 + 2 * hd]
            yd = []
            for hh in range(2):
                h = (c0 // hd) + hh
                seg = acum[:, h:h + 1] - acum_t[h:h + 1, :]
                lmat = jnp.exp(jnp.where(causal, seg, -jnp.inf))
                yd.append(jnp.dot((gmat * lmat).astype(BF16), xpair, preferred_element_type=F32))
            y_parts.append(jnp.where(lane < hd, yd[0], yd[1]))
        hprev = st_ref[:, g * gw:(g + 1) * gw]
        y_off = jnp.dot(cg, hprev.astype(BF16), preferred_element_type=F32) * ea_x[:, g * gw:(g + 1) * gw]
        st_new = lax.dot_general(bg, xdec_b[:, g * gw:(g + 1) * gw], (((0,), (0,)), ((), ())),
                                 preferred_element_type=F32)
        st_ref[:, g * gw:(g + 1) * gw] = hprev * ea_x[q - 1:q, g * gw:(g + 1) * gw] + st_new
        y_parts.append(y_off)
    pp = heads_per_group // 2 + 1
    ys = []
    for g in range(SSD_GROUPS):
        diag = jnp.concatenate(y_parts[g * pp:g * pp + pp - 1], axis=1)
        ys.append(diag + y_parts[g * pp + pp - 1])
    y = jnp.concatenate(ys, axis=1) + xs * dskip_ref[...]
    o_ref[...] = _rms(y * _silu(z_ref[...]), ng_ref[...]).astype(o_ref.dtype)


def _ssd(xbc, z, dt, conv_w, conv_b, dt_bias, a_log, d_skip, norm_g, b, s, *, heads, hd, nstate):
    t, xbcw = xbc.shape
    width = heads * hd
    q = _tile(s, SSD_CHUNK, V7X_LANES)
    nc = s // q
    kconv = conv_w.shape[0]
    cw = jnp.zeros((V7X_SUBLANES, xbcw), F32).at[:kconv].set(conv_w)
    pad = lambda v: jnp.zeros((1, V7X_LANES), F32).at[0, :heads].set(v)
    tri = (jnp.arange(q)[None, :] <= jnp.arange(q)[:, None]).astype(BF16)
    hx = (jnp.arange(V7X_LANES)[:, None] == (jnp.arange(width)[None, :] // hd)).astype(BF16)
    row = lambda i, c: (i * nc + c, 0)
    fix = lambda i, c: (0, 0)
    blocks = (_nbytes((q, xbcw + width + V7X_LANES), F32) + _nbytes((q, width), BF16)
              + _nbytes((q, q), BF16) + _nbytes((V7X_LANES, width), BF16) + 8 * _nbytes((1, xbcw), F32))
    scratch = _nbytes((q + V7X_SUBLANES, xbcw), F32) + _nbytes((nstate, width), F32)
    temps = 24 * _nbytes((q, xbcw), F32)
    return pl.pallas_call(
        functools.partial(_ssd_kernel, q=q, width=width, nstate=nstate, hd=hd, kconv=kconv),
        grid=(b, nc),
        in_specs=[pl.BlockSpec((q, xbcw), row), pl.BlockSpec((q, width), row), pl.BlockSpec((q, V7X_LANES), row),
                  pl.BlockSpec((V7X_SUBLANES, xbcw), fix), pl.BlockSpec((1, xbcw), fix),
                  pl.BlockSpec((1, V7X_LANES), fix), pl.BlockSpec((1, V7X_LANES), fix),
                  pl.BlockSpec((1, width), fix), pl.BlockSpec((1, width), fix),
                  pl.BlockSpec((q, q), fix), pl.BlockSpec((V7X_LANES, width), fix)],
        out_specs=pl.BlockSpec((q, width), row),
        out_shape=jax.ShapeDtypeStruct((t, width), BF16),
        scratch_shapes=[pltpu.VMEM((q + V7X_SUBLANES, xbcw), F32), pltpu.VMEM((nstate, width), F32)],
        compiler_params=_params(("parallel", "arbitrary"), _vmem_limit(blocks, scratch, temps)),
        name="ssd",
    )(xbc, z, dt, cw, conv_b.reshape(1, xbcw), pad(dt_bias), pad(a_log),
      jnp.repeat(d_skip, hd).reshape(1, width), norm_g.reshape(1, width), tri, hx)


def _out_proj_kernel(x_ref, a_ref, y_ref, w_ref, o_ref, *, wa):
    acc = jnp.dot(a_ref[...], w_ref[0:wa, :], preferred_element_type=F32)
    acc = acc + jnp.dot(y_ref[...], w_ref[wa:, :], preferred_element_type=F32)
    o_ref[...] = x_ref[...] + acc


def _out_proj(x2, attn, ynorm, w):
    t, d = x2.shape
    wa, wy = attn.shape[1], ynorm.shape[1]
    tm = _tile(t, ROW_TILE)
    row = lambda i: (i, 0)
    blocks = 2 * _nbytes((tm, d), F32) + _nbytes((tm, wa + wy), BF16) + _nbytes(w.shape, BF16)
    return pl.pallas_call(
        functools.partial(_out_proj_kernel, wa=wa),
        grid=(t // tm,),
        in_specs=[pl.BlockSpec((tm, d), row), pl.BlockSpec((tm, wa), row), pl.BlockSpec((tm, wy), row),
                  pl.BlockSpec(w.shape, lambda i: (0, 0))],
        out_specs=pl.BlockSpec((tm, d), row),
        out_shape=jax.ShapeDtypeStruct((t, d), F32),
        compiler_params=_params(("parallel",), _vmem_limit(blocks, 0, 2 * _nbytes((tm, d), F32))),
        name="out_proj",
    )(x2, attn, ynorm, w)


def _ffn_kernel(h_ref, g_ref, w1_ref, w3_ref, w2_ref, o_ref, n_ref, acc_ref):
    f = pl.program_id(1)

    @pl.when(f == 0)
    def _():
        n_ref[...] = _rms(h_ref[...], g_ref[...]).astype(BF16)
        acc_ref[...] = jnp.zeros_like(acc_ref)

    n = n_ref[...]
    a = _silu(jnp.dot(n, w1_ref[...], preferred_element_type=F32)) * jnp.dot(n, w3_ref[...], preferred_element_type=F32)
    acc_ref[...] += jnp.dot(a.astype(BF16), w2_ref[...], preferred_element_type=F32)

    @pl.when(f == pl.num_programs(1) - 1)
    def _():
        o_ref[...] = h_ref[...] + acc_ref[...]


def _ffn(h, g, w1, w3, w2):
    t, d = h.shape
    ff = w1.shape[1]
    tm = _tile(t, FFN_ROW_TILE)
    tf = _tile(ff, FFN_COL_TILE, V7X_LANES)
    blocks = 2 * _nbytes((tm, d), F32) + 3 * _nbytes((d, tf), BF16)
    scratch = _nbytes((tm, d), BF16) + _nbytes((tm, d), F32)
    temps = 4 * _nbytes((tm, tf), F32) + _nbytes((tm, d), F32)
    return pl.pallas_call(
        _ffn_kernel,
        grid=(t // tm, ff // tf),
        in_specs=[pl.BlockSpec((tm, d), lambda i, f: (i, 0)), pl.BlockSpec((1, d), lambda i, f: (0, 0)),
                  pl.BlockSpec((d, tf), lambda i, f: (0, f)), pl.BlockSpec((d, tf), lambda i, f: (0, f)),
                  pl.BlockSpec((tf, d), lambda i, f: (f, 0))],
        out_specs=pl.BlockSpec((tm, d), lambda i, f: (i, 0)),
        out_shape=jax.ShapeDtypeStruct((t, d), F32),
        scratch_shapes=[pltpu.VMEM((tm, d), BF16), pltpu.VMEM((tm, d), F32)],
        compiler_params=_params(("parallel", "arbitrary"), _vmem_limit(blocks, scratch, temps)),
        name="ffn",
    )(h, g, w1, w3, w2)


def _ple_glu_kernel(h_ref, p_ref, g_ref, wg_ref, wp_ref, g2_ref, w_ref, b_ref, h_out_ref, u_ref, *, c):
    h = h_ref[...]
    n = _rms(h, g_ref[...]).astype(BF16)
    gate = _sigmoid(jnp.dot(n, wg_ref[...], preferred_element_type=F32))
    emb = jnp.dot(p_ref[...].astype(BF16), wp_ref[...], preferred_element_type=F32)
    h = h + emb * gate
    h_out_ref[...] = h
    n2 = _rms(h, g2_ref[...]).astype(BF16)
    val = jnp.dot(n2, w_ref[:, 0:c], preferred_element_type=F32) + b_ref[:, 0:c]
    glu_gate = jnp.dot(n2, w_ref[:, c:], preferred_element_type=F32) + b_ref[:, c:]
    u_ref[...] = val * _sigmoid(glu_gate)


def _ple_glu(h, p3, layer, g, w_gate, w_ple, g2, w, bias):
    t, d = h.shape
    pd = p3.shape[2]
    c = w.shape[1] // 2
    tm = _tile(t, ROW_TILE)
    row = lambda i: (i, 0)
    fix = lambda i: (0, 0)
    blocks = (2 * _nbytes((tm, d), F32) + _nbytes((tm, pd), F32) + _nbytes((tm, c), F32)
              + _nbytes((d + pd, d), BF16) + _nbytes(w.shape, BF16))
    return pl.pallas_call(
        functools.partial(_ple_glu_kernel, c=c),
        grid=(t // tm,),
        in_specs=[pl.BlockSpec((tm, d), row), pl.BlockSpec((None, tm, pd), lambda i: (layer, i, 0)),
                  pl.BlockSpec((1, d), fix), pl.BlockSpec((d, d), fix), pl.BlockSpec((pd, d), fix),
                  pl.BlockSpec((1, d), fix), pl.BlockSpec(w.shape, fix), pl.BlockSpec((1, 2 * c), fix)],
        out_specs=[pl.BlockSpec((tm, d), row), pl.BlockSpec((tm, c), row)],
        out_shape=[jax.ShapeDtypeStruct((t, d), F32), jax.ShapeDtypeStruct((t, c), F32)],
        compiler_params=_params(("parallel",), _vmem_limit(blocks, 0, 6 * _nbytes((tm, d), F32))),
        name="ple_glu",
    )(h, p3, g, w_gate, w_ple, g2, w, bias)


def _dwconv_kernel(u_ref, h_ref, dw_ref, db_ref, lg_ref, lb_ref, w_ref, b_ref, rg_ref, wr_ref, tri_ref,
                   o_ref, meta_ref, cnt_ref, ext_ref, sh_ref, cv_ref, run_ref, *, ts, kconv, rc, ne):
    halo = CONV_HALO
    sub = V7X_SUBLANES

    @pl.when((pl.program_id(0) == 0) & (pl.program_id(1) == 0))
    def _():
        run_ref[...] = jnp.zeros_like(run_ref)

    @pl.when(pl.program_id(1) == 0)
    def _():
        ext_ref[0:halo, :] = jnp.zeros((halo, ext_ref.shape[1]), F32)

    ext_ref[halo:halo + ts, :] = u_ref[...]
    base = halo - (kconv - 1)
    for r in range(1, sub):
        sh_ref[r - 1] = ext_ref[pl.ds(r, sh_ref.shape[1]), :]

    def tap(k, r0):
        a, r = divmod(base + k, sub)
        start = pl.multiple_of(r0 + a * sub, sub)
        if r == 0:
            return ext_ref[pl.ds(start, rc), :]
        return sh_ref[r - 1, pl.ds(start, rc), :]

    def chunk(i, c):
        r0 = pl.multiple_of(i * rc, rc)
        acc = db_ref[...] + dw_ref[0:1, :] * tap(0, r0)
        for k in range(1, kconv):
            acc = acc + dw_ref[k:k + 1, :] * tap(k, r0)
        cv_ref[pl.ds(r0, rc), :] = acc
        return c

    lax.fori_loop(0, ts // rc, chunk, 0)
    ext_ref[0:halo, :] = ext_ref[ts:ts + halo, :]

    cv = cv_ref[...]
    mu = jnp.mean(cv, axis=-1, keepdims=True)
    xc = cv - mu
    var = jnp.mean(xc * xc, axis=-1, keepdims=True)
    ln = xc * lax.rsqrt(var + EPS) * lg_ref[...] + lb_ref[...]
    act = _silu(ln).astype(BF16)
    h_new = h_ref[...] + jnp.dot(act, w_ref[...], preferred_element_type=F32) + b_ref[...]
    o_ref[...] = h_new
    _route_rows(h_new, rg_ref, wr_ref, tri_ref, run_ref, meta_ref, cnt_ref, ne)


def _dwconv_route(u, h, dw_w, dw_b, ln_g, ln_b, w2, b2, route_g, wr_pad, ne, b, s):
    t, c = u.shape
    d = h.shape[1]
    kconv = dw_w.shape[0]
    assert kconv - 1 <= CONV_HALO
    ts = _tile(s, CONV_ROW_TILE)
    rc = _tile(ts, CONV_ROW_CHUNK)
    nt = s // ts
    kp = -(-kconv // V7X_SUBLANES) * V7X_SUBLANES
    dwp = jnp.zeros((kp, c), F32).at[:kconv].set(dw_w)
    tri = (jnp.arange(ts)[None, :] < jnp.arange(ts)[:, None]).astype(BF16)
    row = lambda i, j: (i * nt + j, 0)
    fix = lambda i, j: (0, 0)
    blocks = (_nbytes((ts, c), F32) + 3 * _nbytes((ts, d), F32) + _nbytes((c, d), BF16) + _nbytes((kp, c), F32)
              + _nbytes((ts, V7X_LANES), F32) + _nbytes(wr_pad.shape, BF16) + _nbytes((ts, ts), BF16))
    sh_rows = ts + CONV_HALO - V7X_SUBLANES
    scratch = _nbytes((ts + CONV_HALO, c), F32) + _nbytes((ts, c), F32) + _nbytes((V7X_SUBLANES - 1, sh_rows, c), F32)
    return pl.pallas_call(
        functools.partial(_dwconv_kernel, ts=ts, kconv=kconv, rc=rc, ne=ne),
        grid=(b, nt),
        in_specs=[pl.BlockSpec((ts, c), row), pl.BlockSpec((ts, d), row), pl.BlockSpec((kp, c), fix),
                  pl.BlockSpec((1, c), fix), pl.BlockSpec((1, c), fix), pl.BlockSpec((1, c), fix),
                  pl.BlockSpec((c, d), fix), pl.BlockSpec((1, d), fix),
                  pl.BlockSpec((1, d), fix), pl.BlockSpec(wr_pad.shape, fix), pl.BlockSpec((ts, ts), fix)],
        out_specs=[pl.BlockSpec((ts, d), row), pl.BlockSpec((ts, V7X_LANES), row),
                   pl.BlockSpec((V7X_SUBLANES, V7X_LANES), fix)],
        out_shape=[jax.ShapeDtypeStruct((t, d), F32), jax.ShapeDtypeStruct((t, V7X_LANES), F32),
                   jax.ShapeDtypeStruct((V7X_SUBLANES, V7X_LANES), F32)],
        scratch_shapes=[pltpu.VMEM((ts + CONV_HALO, c), F32), pltpu.VMEM((V7X_SUBLANES - 1, sh_rows, c), F32),
                        pltpu.VMEM((ts, c), F32), pltpu.VMEM((1, V7X_LANES), F32)],
        compiler_params=_params(("arbitrary", "arbitrary"), _vmem_limit(blocks, scratch, 8 * _nbytes((ts, c), F32))),
        name="dwconv_route",
    )(u, h, dwp, dw_b.reshape(1, c), ln_g.reshape(1, c), ln_b.reshape(1, c), w2, b2.reshape(1, d),
      route_g, wr_pad, tri)


def _route_rows(h, g_ref, wr_ref, tri_ref, run_ref, meta_ref, cnt_ref, ne):
    nf = _rms(h, g_ref[...])
    logits = jnp.dot(nf.astype(BF16), wr_ref[...], preferred_element_type=F32)
    lane = lax.broadcasted_iota(jnp.int32, logits.shape, 1)
    neg = jnp.float32(-jnp.inf)
    logits = jnp.where(lane < ne, logits, neg)
    big = jnp.int32(V7X_LANES)
    m1 = jnp.max(logits, axis=-1, keepdims=True)
    i1 = jnp.min(jnp.where(logits == m1, lane, big), axis=-1, keepdims=True)
    sel1 = lane == i1
    rest = jnp.where(sel1, neg, logits)
    m2 = jnp.max(rest, axis=-1, keepdims=True)
    i2 = jnp.min(jnp.where(rest == m2, lane, big), axis=-1, keepdims=True)
    sel2 = lane == i2
    e2 = jnp.exp(m2 - m1)
    g1 = 1.0 / (1.0 + e2)
    g2 = e2 / (1.0 + e2)
    onehot = jnp.where(sel1 | sel2, 1.0, 0.0)
    before = jnp.dot(tri_ref[...], onehot.astype(BF16), preferred_element_type=F32) + run_ref[...]
    r1 = jnp.sum(jnp.where(sel1, before, 0.0), axis=-1, keepdims=True)
    r2 = jnp.sum(jnp.where(sel2, before, 0.0), axis=-1, keepdims=True)
    rows = onehot.shape[0]
    total = before[rows - 1:rows, :] + onehot[rows - 1:rows, :]
    run_ref[...] = total
    cnt_ref[...] = jnp.broadcast_to(total, cnt_ref.shape)
    meta = jnp.where(lane == 0, g1, jnp.where(lane == 1, g2, 0.0))
    meta = jnp.where(lane == 2, i1.astype(F32), jnp.where(lane == 3, i2.astype(F32), meta))
    meta = jnp.where(lane == 4, r1, jnp.where(lane == 5, r2, meta))
    meta_ref[...] = meta


def _moe_plan(meta, counts_f, tm, ne):
    t = meta.shape[0]
    nt = TOP_K * t // tm + ne
    e_sel = meta[:, 2:2 + TOP_K].astype(jnp.int32)
    rank = meta[:, 2 + TOP_K:2 + 2 * TOP_K].astype(jnp.int32)
    counts = counts_f[0, :ne].astype(jnp.int32)
    padded = ((counts + tm - 1) // tm) * tm
    gend = jnp.cumsum(padded)
    gstart = gend - padded
    slot = gstart[e_sel] + rank
    tile_start = jnp.arange(nt, dtype=jnp.int32) * tm
    tile_e = jnp.sum((tile_start[:, None] >= gend[None, :]).astype(jnp.int32), axis=1)
    valid = (tile_e < ne).astype(jnp.int32)
    return slot, jnp.minimum(tile_e, ne - 1), valid, nt * tm


def _sc_geometry(t, d, dtype):
    sc = pltpu.get_tpu_info().sparse_core
    nc, nw = sc.num_cores, sc.num_cores * sc.num_subcores
    per_w = t // nw
    chunk = _tile(per_w, SC_GATHER_ROWS)
    assert t % nw == 0 and per_w % chunk == 0 and chunk % V7X_SUBLANES == 0
    assert TOP_K * chunk * (d * jnp.dtype(dtype).itemsize + 4) <= sc.vmem_capacity_bytes
    mesh = plsc.VectorSubcoreMesh(core_axis_name="c", subcore_axis_name="s")
    return nc, per_w, chunk, mesh


def _dispatch_rows(rows, slot0, slot1, n_slots):
    t, d = rows.shape
    nc, per_w, chunk, mesh = _sc_geometry(t, d, rows.dtype)

    @functools.partial(
        pl.kernel, mesh=mesh, out_type=jax.ShapeDtypeStruct((n_slots, d), rows.dtype),
        scratch_types=[pltpu.VMEM((chunk,), jnp.int32), pltpu.VMEM((chunk,), jnp.int32),
                       pltpu.VMEM((chunk, d), rows.dtype)])
    def dispatch(rows_hbm, s0_hbm, s1_hbm, out_hbm, i0_v, i1_v, rows_v):
        base = (lax.axis_index("s") * nc + lax.axis_index("c")) * per_w

        @pl.loop(0, per_w // chunk)
        def _(c):
            off = pl.multiple_of(base + c * chunk, chunk)
            pltpu.sync_copy(s0_hbm.at[pl.ds(off, chunk)], i0_v)
            pltpu.sync_copy(s1_hbm.at[pl.ds(off, chunk)], i1_v)
            pltpu.sync_copy(rows_hbm.at[pl.ds(off, chunk)], rows_v)
            pltpu.sync_copy(rows_v, out_hbm.at[i0_v])
            pltpu.sync_copy(rows_v, out_hbm.at[i1_v])

    return dispatch(rows, slot0, slot1)


def _combine_rows(table, slot0, slot1):
    t = slot0.shape[0]
    d = table.shape[1]
    nc, per_w, chunk, mesh = _sc_geometry(t, d, table.dtype)
    out = jax.ShapeDtypeStruct((t, d), table.dtype)

    @functools.partial(
        pl.kernel, mesh=mesh, out_type=(out, out),
        scratch_types=[pltpu.VMEM((chunk,), jnp.int32), pltpu.VMEM((chunk,), jnp.int32),
                       pltpu.VMEM((chunk, d), table.dtype), pltpu.VMEM((chunk, d), table.dtype),
                       pltpu.SemaphoreType.DMA, pltpu.SemaphoreType.DMA])
    def combine(table_hbm, s0_hbm, s1_hbm, y0_hbm, y1_hbm, i0_v, i1_v, r0_v, r1_v, sem0, sem1):
        base = (lax.axis_index("s") * nc + lax.axis_index("c")) * per_w

        @pl.loop(0, per_w // chunk)
        def _(c):
            off = pl.multiple_of(base + c * chunk, chunk)
            pltpu.sync_copy(s0_hbm.at[pl.ds(off, chunk)], i0_v)
            pltpu.sync_copy(s1_hbm.at[pl.ds(off, chunk)], i1_v)
            g0 = pltpu.async_copy(table_hbm.at[i0_v], r0_v, sem0)
            g1 = pltpu.async_copy(table_hbm.at[i1_v], r1_v, sem1)
            g0.wait()
            pltpu.sync_copy(r0_v, y0_hbm.at[pl.ds(off, chunk)])
            g1.wait()
            pltpu.sync_copy(r1_v, y1_hbm.at[pl.ds(off, chunk)])

    return combine(table, slot0, slot1)


def _moe_kernel(te_ref, tv_ref, x_ref, g_ref, w1_ref, w3_ref, w2_ref, o_ref, n_ref, acc_ref):
    i = pl.program_id(0)
    f = pl.program_id(1)
    last = pl.num_programs(1) - 1
    valid = tv_ref[i] == 1

    @pl.when(valid & (f == 0))
    def _():
        n_ref[...] = _rms(x_ref[...], g_ref[...]).astype(BF16)
        acc_ref[...] = jnp.zeros_like(acc_ref)

    @pl.when(valid)
    def _():
        n = n_ref[...]
        w1 = w1_ref[0].astype(BF16)
        w3 = w3_ref[0].astype(BF16)
        w2 = w2_ref[0].astype(BF16)
        a = _silu(jnp.dot(n, w1, preferred_element_type=F32)) * jnp.dot(n, w3, preferred_element_type=F32)
        acc_ref[...] += jnp.dot(a.astype(BF16), w2, preferred_element_type=F32)

    @pl.when(valid & (f == last))
    def _():
        o_ref[...] = acc_ref[...]

    @pl.when(jnp.logical_not(valid) & (f == last))
    def _():
        o_ref[...] = jnp.zeros_like(o_ref)


def _moe_experts(xs, g, tile_e, tile_valid, w1, w3, w2, tm):
    ns, d = xs.shape
    ne, _, ff = w1.shape
    tf = _tile(ff, MOE_COL_TILE, V7X_LANES)
    nf = ff // tf
    fidx = lambda f, tv, i: f * tv[i] + (nf - 1) * (1 - tv[i])
    blocks = 2 * _nbytes((tm, d), F32) + 3 * _nbytes((d, tf), w1.dtype)
    scratch = _nbytes((tm, d), BF16) + _nbytes((tm, d), F32)
    temps = 4 * _nbytes((tm, tf), F32) + _nbytes((tm, d), F32) + 3 * _nbytes((d, tf), BF16)
    grid_spec = pltpu.PrefetchScalarGridSpec(
        num_scalar_prefetch=2,
        grid=(ns // tm, nf),
        in_specs=[pl.BlockSpec((tm, d), lambda i, f, te, tv: (i, 0)),
                  pl.BlockSpec((1, d), lambda i, f, te, tv: (0, 0)),
                  pl.BlockSpec((1, d, tf), lambda i, f, te, tv: (te[i], 0, fidx(f, tv, i))),
                  pl.BlockSpec((1, d, tf), lambda i, f, te, tv: (te[i], 0, fidx(f, tv, i))),
                  pl.BlockSpec((1, tf, d), lambda i, f, te, tv: (te[i], fidx(f, tv, i), 0))],
        out_specs=pl.BlockSpec((tm, d), lambda i, f, te, tv: (i, 0)),
        scratch_shapes=[pltpu.VMEM((tm, d), BF16), pltpu.VMEM((tm, d), F32)])
    return pl.pallas_call(
        _moe_kernel,
        grid_spec=grid_spec,
        out_shape=jax.ShapeDtypeStruct((ns, d), F32),
        compiler_params=_params(("arbitrary", "arbitrary"), _vmem_limit(blocks, scratch, temps)),
        name="moe_experts",
    )(tile_e, tile_valid, xs, g, w1, w3, w2)


def _combine_ple_kernel(h_ref, y0_ref, y1_ref, m_ref, p_ref, g_ref, wg_ref, wp_ref, fg_ref, o_ref):
    meta = m_ref[...]
    h = h_ref[...] + meta[:, 0:1] * y0_ref[...] + meta[:, 1:2] * y1_ref[...]
    n = _rms(h, g_ref[...]).astype(BF16)
    gate = _sigmoid(jnp.dot(n, wg_ref[...], preferred_element_type=F32))
    emb = jnp.dot(p_ref[...].astype(BF16), wp_ref[...], preferred_element_type=F32)
    o_ref[...] = _rms(h + emb * gate, fg_ref[...])


def _combine_ple(h, y0, y1, meta, p3, layer, g, w_gate, w_ple, final_g):
    t, d = h.shape
    pd = p3.shape[2]
    tm = _tile(t, ROW_TILE)
    row = lambda i: (i, 0)
    fix = lambda i: (0, 0)
    blocks = (4 * _nbytes((tm, d), F32) + _nbytes((tm, pd + V7X_LANES), F32) + _nbytes((d + pd, d), BF16))
    return pl.pallas_call(
        _combine_ple_kernel,
        grid=(t // tm,),
        in_specs=[pl.BlockSpec((tm, d), row), pl.BlockSpec((tm, d), row), pl.BlockSpec((tm, d), row),
                  pl.BlockSpec((tm, V7X_LANES), row),
                  pl.BlockSpec((None, tm, pd), lambda i: (layer, i, 0)), pl.BlockSpec((1, d), fix),
                  pl.BlockSpec((d, d), fix), pl.BlockSpec((pd, d), fix), pl.BlockSpec((1, d), fix)],
        out_specs=pl.BlockSpec((tm, d), row),
        out_shape=jax.ShapeDtypeStruct((t, d), F32),
        compiler_params=_params(("parallel",), _vmem_limit(blocks, 0, 4 * _nbytes((tm, d), F32))),
        name="combine_ple_final",
    )(h, y0, y1, meta, p3, g, w_gate, w_ple, final_g)


def kernel(x, p, mix_norm_g, ffn_norm_g, ple_norm_g, w_ple, w_ple_gate, ev_w_in, ev_conv_w, ev_conv_b, ev_dt_bias, ev_a_log, ev_d, ev_ssd_norm_g, ev_w_out, ev_ffn_w1, ev_ffn_w3, ev_ffn_w2, od_pw1_w, od_pw1_b, od_dw_w, od_dw_b, od_ln_g, od_ln_b, od_pw2_w, od_pw2_b, od_router_w, od_moe_w1, od_moe_w3, od_moe_w2, final_norm_g):
    b, s, d = x.shape
    t = b * s
    depth = p.shape[0]
    assert depth == 2 and ev_w_in.shape[0] == 1 and od_pw1_w.shape[0] == 1
    sbw = SB_HEADS * SB_HEAD_DIM
    ssdw = SSD_HEADS * SSD_HEAD_DIM
    xbcw = ssdw + 2 * SSD_GROUPS * SSD_STATE
    in_width = 3 * sbw + ssdw + xbcw + SSD_HEADS
    assert ev_w_in.shape[2] == in_width
    bf = lambda w: w.astype(BF16)
    vec = lambda v: v.reshape(1, -1)

    x2 = x.reshape(t, d)
    p3 = p.reshape(depth, t, -1)

    w_all = bf(ev_w_in[0])
    w_rest = w_all[:, 3 * sbw:]
    w_in = jnp.concatenate([w_all[:, :2 * sbw], w_rest,
                            jnp.zeros((d, V7X_LANES - SSD_HEADS), BF16)], axis=1)
    w_vt = w_all[:, 2 * sbw:3 * sbw].T
    qscale = LOG2E / math.sqrt(SB_HEAD_DIM)
    qk, vt, z, xbc, dt = _in_proj(x2, vec(mix_norm_g[0]), w_in, w_vt, sbw=sbw, zw=ssdw, xbcw=xbcw, qscale=qscale)
    attn = _sb_attention(qk, vt, b, s, heads=SB_HEADS, hd=SB_HEAD_DIM)
    ynorm = _ssd(xbc, z, dt, ev_conv_w[0], ev_conv_b[0], ev_dt_bias[0], ev_a_log[0], ev_d[0], ev_ssd_norm_g[0],
                 b, s, heads=SSD_HEADS, hd=SSD_HEAD_DIM, nstate=SSD_STATE)
    h = _out_proj(x2, attn, ynorm, bf(ev_w_out[0]))
    h = _ffn(h, vec(ffn_norm_g[0]), bf(ev_ffn_w1[0]), bf(ev_ffn_w3[0]), bf(ev_ffn_w2[0]))
    h, u = _ple_glu(h, p3, 0, vec(ple_norm_g[0]), bf(w_ple_gate[0]), bf(w_ple[0]),
                    vec(mix_norm_g[1]), bf(od_pw1_w[0]), vec(od_pw1_b[0]))
    ne = od_router_w.shape[2]
    wr = jnp.zeros((d, V7X_LANES), BF16).at[:, :ne].set(bf(od_router_w[0]))
    h, meta, counts = _dwconv_route(u, h, od_dw_w[0], od_dw_b[0], od_ln_g[0], od_ln_b[0], bf(od_pw2_w[0]),
                                       od_pw2_b[0], vec(ffn_norm_g[1]), wr, ne, b, s)
    tm = _tile(TOP_K * t, FFN_ROW_TILE)
    slot, tile_e, tile_valid, n_slots = _moe_plan(meta, counts, tm, ne)
    xs = _dispatch_rows(h, slot[:, 0], slot[:, 1], n_slots)
    ys = _moe_experts(xs, vec(ffn_norm_g[1]), tile_e, tile_valid, od_moe_w1[0], od_moe_w3[0], od_moe_w2[0], tm)
    y0, y1 = _combine_rows(ys, slot[:, 0], slot[:, 1])
    out = _combine_ple(h, y0, y1, meta, p3, 1, vec(ple_norm_g[1]), bf(w_ple_gate[1]), bf(w_ple[1]), vec(final_norm_g))
    return out.reshape(b, s, d)
```

```python
import functools
import math

import jax
import jax.numpy as jnp
from jax import lax
from jax.experimental import pallas as pl
from jax.experimental.pallas import tpu as pltpu
from jax.experimental.pallas import tpu_sc as plsc

F32 = jnp.float32
BF16 = jnp.bfloat16
EPS = 1e-6
LOG2E = 1.4426950408889634

V7X_LANES = 128
V7X_SUBLANES = 8
V7X_VMEM_BYTES = 64 * 1024 * 1024

SB_HEADS = 8
SB_HEAD_DIM = 64
SSD_HEADS = 8
SSD_HEAD_DIM = 64
SSD_STATE = 128
SSD_GROUPS = 2
SSD_CONV = 4
N_EXPERTS = 8
TOP_K = 2

ROW_TILE = 1024
FFN_ROW_TILE = 1024
FFN_COL_TILE = 1408
MOE_COL_TILE = 512
ATTN_BLOCK = 256
ATTN_PAIRS = 2
SSD_CHUNK = 128
CONV_ROW_TILE = 256
CONV_ROW_CHUNK = 32
CONV_HALO = 32
SC_GATHER_ROWS = 32


def _tile(n, pref, mult=V7X_SUBLANES):
    if n <= pref:
        return n
    t = (pref // mult) * mult
    while t > mult and n % t:
        t -= mult
    assert n % t == 0, (n, pref, mult)
    return t


def _vmem_limit(block_bytes, scratch_bytes=0, temp_bytes=0):
    need = 2 * block_bytes + scratch_bytes + temp_bytes + (4 << 20)
    return int(min(max(need, 16 << 20), V7X_VMEM_BYTES - (6 << 20)))


def _params(sem, vmem):
    return pltpu.CompilerParams(dimension_semantics=sem, vmem_limit_bytes=vmem)


def _nbytes(shape, dtype):
    return math.prod(shape) * jnp.dtype(dtype).itemsize


def _rms(x, g):
    return x * lax.rsqrt(jnp.mean(x * x, axis=-1, keepdims=True) + EPS) * g


def _sigmoid(x):
    return 1.0 / (1.0 + jnp.exp(-x))


def _silu(x):
    return x * _sigmoid(x)


def _softplus(x):
    return jnp.maximum(x, 0.0) + jnp.log(1.0 + jnp.exp(-jnp.abs(x)))


def _split3(x):
    hi = x.astype(BF16)
    r1 = x - hi.astype(F32)
    mid = r1.astype(BF16)
    lo = (r1 - mid.astype(F32)).astype(BF16)
    return hi, mid, lo


def _dot_exact_rhs(x, m):
    hi, mid, lo = _split3(x)
    d = functools.partial(jnp.dot, preferred_element_type=F32)
    return d(hi, m) + d(mid, m) + d(lo, m)


def _dot_exact_lhs(m, x):
    hi, mid, lo = _split3(x)
    d = functools.partial(jnp.dot, preferred_element_type=F32)
    return d(m, hi) + d(m, mid) + d(m, lo)


def _in_proj_kernel(x_ref, g_ref, w_ref, wvt_ref, qk_ref, vt_ref, z_ref, xbc_ref, dt_ref, *, sbw, zw, xbcw, qscale):
    hn = _rms(x_ref[...], g_ref[...]).astype(BF16)

    def mm(lo, hi):
        return jnp.dot(hn, w_ref[:, lo:hi], preferred_element_type=F32)

    qk_ref[:, 0:sbw] = (mm(0, sbw) * qscale).astype(BF16)
    qk_ref[:, sbw:2 * sbw] = mm(sbw, 2 * sbw).astype(BF16)
    vt_ref[...] = lax.dot_general(wvt_ref[...], hn, (((1,), (1,)), ((), ())),
                                  preferred_element_type=F32).astype(BF16)
    o = 2 * sbw
    z_ref[...] = mm(o, o + zw)
    o += zw
    xbc_ref[...] = mm(o, o + xbcw)
    o += xbcw
    dt_ref[...] = mm(o, o + V7X_LANES)


def _in_proj(x2, g, w_pad, w_vt, *, sbw, zw, xbcw, qscale):
    t, d = x2.shape
    n = w_pad.shape[1]
    tm = _tile(t, ROW_TILE, V7X_LANES)
    blocks = (_nbytes((tm, d), F32) + _nbytes((d, n + sbw), BF16) + _nbytes((tm, 3 * sbw), BF16)
              + _nbytes((tm, zw + xbcw + V7X_LANES), F32))
    return pl.pallas_call(
        functools.partial(_in_proj_kernel, sbw=sbw, zw=zw, xbcw=xbcw, qscale=qscale),
        grid=(t // tm,),
        in_specs=[pl.BlockSpec((tm, d), lambda i: (i, 0)),
                  pl.BlockSpec((1, d), lambda i: (0, 0)),
                  pl.BlockSpec((d, n), lambda i: (0, 0)),
                  pl.BlockSpec((sbw, d), lambda i: (0, 0))],
        out_specs=[pl.BlockSpec((tm, 2 * sbw), lambda i: (i, 0)),
                   pl.BlockSpec((sbw, tm), lambda i: (0, i)),
                   pl.BlockSpec((tm, zw), lambda i: (i, 0)),
                   pl.BlockSpec((tm, xbcw), lambda i: (i, 0)),
                   pl.BlockSpec((tm, V7X_LANES), lambda i: (i, 0))],
        out_shape=[jax.ShapeDtypeStruct((t, 2 * sbw), BF16),
                   jax.ShapeDtypeStruct((sbw, t), BF16),
                   jax.ShapeDtypeStruct((t, zw), F32),
                   jax.ShapeDtypeStruct((t, xbcw), F32),
                   jax.ShapeDtypeStruct((t, V7X_LANES), F32)],
        compiler_params=_params(("parallel",), _vmem_limit(blocks, 0, _nbytes((tm, n), F32))),
        name="in_proj",
    )(x2, g, w_pad, w_vt)


ATTN_MASKED = -1e30


def _sb_attn_kernel(q_ref, k_ref, vt_ref, m_ref, o_ref, *scratch, blk, hd, npair):
    qi = pl.program_id(2)
    nblocks = qi + 1
    width = 2 * hd
    nh = 2 * npair
    acc_refs, car_refs, z_refs, incl_refs = (scratch[i * nh:(i + 1) * nh] for i in range(4))
    lane = lax.broadcasted_iota(jnp.int32, (1, width), 1)
    qh = []
    for p in range(npair):
        q = q_ref[:, p * width:(p + 1) * width]
        qz = jnp.zeros_like(q)
        qh += [jnp.where(lane < hd, q, qz), jnp.where(lane >= hd, q, qz)]
    sign = jnp.uint32(0x80000000)

    def key_start(m):
        j = jnp.where(m < nblocks, qi - m, qi)
        return pl.multiple_of(j * blk, blk)

    def scores(m, h):
        p = h // 2
        kblk = k_ref[pl.ds(key_start(m), blk), p * width:(p + 1) * width]
        return lax.dot_general(kblk, qh[h], (((1,), (1,)), ((), ())), preferred_element_type=F32)

    def cumsum_stage(h, z_cur, i_cur):
        nabs = pltpu.bitcast(pltpu.bitcast(z_cur, jnp.uint32) | sign, F32)
        sp2 = jnp.maximum(z_cur, 0.0) + jnp.log2(1.0 + jnp.exp2(nabs))
        incl_refs[h][i_cur] = jnp.dot(m_ref[...], sp2.astype(BF16), preferred_element_type=F32)

    def weight_stage(h, z_prev, incl, vstart):
        p = h // 2
        car = car_refs[h][...]
        w = jnp.exp2(z_prev - incl - car)
        vtblk = vt_ref[p * width:(p + 1) * width, pl.ds(vstart, blk)]
        acc_refs[h][...] += jnp.dot(vtblk, w.astype(BF16), preferred_element_type=F32)
        car_refs[h][...] = car + incl[0:1, :]

    row = lax.broadcasted_iota(jnp.int32, (blk, blk), 0)
    col = lax.broadcasted_iota(jnp.int32, (blk, blk), 1)
    for h in range(nh):
        acc_refs[h][...] = jnp.zeros_like(acc_refs[h])
        car_refs[h][...] = jnp.zeros_like(car_refs[h])
        z_refs[h][0] = jnp.where(row < col, scores(0, h), ATTN_MASKED)

    def first(m, c):
        for h in range(nh):
            cumsum_stage(h, z_refs[h][0], 0)
            z_refs[h][1] = scores(1, h)
        return c

    lax.fori_loop(0, jnp.minimum(nblocks, 1), first, 0)

    def body(m, c):
        s_next = lax.rem(m + 1, 3)
        s_cur = lax.rem(m, 3)
        s_prev = lax.rem(m + 2, 3)
        i_cur = lax.rem(m, 2)
        i_prev = lax.rem(m + 1, 2)
        vstart = key_start(m - 1)
        for h in range(nh):
            z_prev = z_refs[h][s_prev]
            incl = incl_refs[h][i_prev]
            z_cur = z_refs[h][s_cur]
            weight_stage(h, z_prev, incl, vstart)
            cumsum_stage(h, z_cur, i_cur)
            z_refs[h][s_next] = scores(m + 1, h)
        return c

    lax.fori_loop(1, nblocks, body, 0)
    last = nblocks - 1
    for h in range(nh):
        weight_stage(h, z_refs[h][lax.rem(last, 3)], incl_refs[h][lax.rem(last, 2)], key_start(last))
    srow = lax.broadcasted_iota(jnp.int32, (width, 1), 0)
    for p in range(npair):
        both = jnp.where(srow < hd, acc_refs[2 * p][...], acc_refs[2 * p + 1][...])
        o_ref[:, p * width:(p + 1) * width] = both.T.astype(o_ref.dtype)


def _sb_attention(qk, vt, b, s, *, heads, hd):
    t = qk.shape[0]
    blk = _tile(s, ATTN_BLOCK, V7X_LANES)
    nq = s // blk
    width = 2 * hd
    assert width == V7X_LANES and heads % (2 * ATTN_PAIRS) == 0
    npair = ATTN_PAIRS
    gw = npair * width
    ng = heads * hd // gw
    m_t = (jnp.arange(blk)[None, :] >= jnp.arange(blk)[:, None]).astype(BF16)
    blocks = 2 * _nbytes((blk, gw), BF16) + 2 * _nbytes((s, gw), BF16) + _nbytes((blk, blk), BF16)
    nh = 2 * npair
    scratch = nh * (_nbytes((width, blk), F32) + _nbytes((V7X_SUBLANES, blk), F32) + 5 * _nbytes((blk, blk), F32))
    temps = 4 * nh * _nbytes((blk, blk), F32)
    return pl.pallas_call(
        functools.partial(_sb_attn_kernel, blk=blk, hd=hd, npair=npair),
        grid=(b, ng, nq),
        in_specs=[pl.BlockSpec((blk, gw), lambda bi, g, qi: (bi * nq + qi, g)),
                  pl.BlockSpec((s, gw), lambda bi, g, qi: (bi, ng + g)),
                  pl.BlockSpec((gw, s), lambda bi, g, qi: (g, bi)),
                  pl.BlockSpec((blk, blk), lambda bi, g, qi: (0, 0))],
        out_specs=pl.BlockSpec((blk, gw), lambda bi, g, qi: (bi * nq + qi, g)),
        out_shape=jax.ShapeDtypeStruct((t, heads * hd), BF16),
        scratch_shapes=([pltpu.VMEM((width, blk), F32)] * nh + [pltpu.VMEM((1, blk), F32)] * nh
                        + [pltpu.VMEM((3, blk, blk), F32)] * nh + [pltpu.VMEM((2, blk, blk), F32)] * nh),
        compiler_params=_params(("parallel", "parallel", "arbitrary"), _vmem_limit(blocks, scratch, temps)),
        name="sb_attention",
    )(qk, qk, vt, m_t)


def _ssd_kernel(xbc_ref, z_ref, dt_ref, cw_ref, cb_ref, dtb_ref, alog_ref, dskip_ref, ng_ref, tri_ref, exp_ref,
                o_ref, ext_ref, st_ref, *, q, width, nstate, hd, kconv):
    halo = V7X_SUBLANES
    gw = width // SSD_GROUPS

    @pl.when(pl.program_id(1) == 0)
    def _():
        ext_ref[0:halo, :] = jnp.zeros((halo, ext_ref.shape[1]), F32)
        st_ref[...] = jnp.zeros_like(st_ref)

    x_new = xbc_ref[...]
    ext_ref[halo:halo + q, :] = x_new
    conv = cb_ref[...] + cw_ref[0:1, :] * ext_ref[pl.ds(halo - (kconv - 1), q), :]
    for k in range(1, kconv):
        conv = conv + cw_ref[k:k + 1, :] * ext_ref[pl.ds(halo - (kconv - 1) + k, q), :]
    ext_ref[0:halo, :] = x_new[q - halo:q, :]
    xbc = _silu(conv)
    xs = xbc[:, :width]
    bm = xbc[:, width:width + SSD_GROUPS * nstate]
    cm = xbc[:, width + SSD_GROUPS * nstate:]

    dt = _softplus(dt_ref[...] + dtb_ref[...])
    a = -jnp.exp(alog_ref[...])
    acum = _dot_exact_lhs(tri_ref[...], dt * a)
    ea = jnp.exp(acum)
    dte = jnp.exp(acum[q - 1:q, :] - acum)
    ex = exp_ref[...]
    dt_x = _dot_exact_rhs(dt, ex)
    ea_x = _dot_exact_rhs(ea, ex)
    dte_x = _dot_exact_rhs(dte, ex)
    xdt = xs * dt_x
    xdt_b = xdt.astype(BF16)
    xdec_b = (xdt * dte_x).astype(BF16)
    acum_t = acum.T

    row = lax.broadcasted_iota(jnp.int32, (q, q), 0)
    col = lax.broadcasted_iota(jnp.int32, (q, q), 1)
    causal = col <= row
    lane = lax.broadcasted_iota(jnp.int32, (1, 2 * hd), 1)
    heads_per_group = gw // hd
    y_parts = []
    for g in range(SSD_GROUPS):
        bg = bm[:, g * nstate:(g + 1) * nstate].astype(BF16)
        cg = cm[:, g * nstate:(g + 1) * nstate].astype(BF16)
        gmat = lax.dot_general(cg, bg, (((1,), (1,)), ((), ())), preferred_element_type=F32)
        for pr in range(heads_per_group // 2):
            c0 = g * gw + pr * 2 * hd
            xpair = xdt_b[:, c0:c0 + 2 * hd]
            yd = []
            for hh in range(2):
                h = (c0 // hd) + hh
                seg = acum[:, h:h + 1] - acum_t[h:h + 1, :]
                lmat = jnp.exp(jnp.where(causal, seg, -jnp.inf))
                yd.append(jnp.dot((gmat * lmat).astype(BF16), xpair, preferred_element_type=F32))
            y_parts.append(jnp.where(lane < hd, yd[0], yd[1]))
        hprev = st_ref[:, g * gw:(g + 1) * gw]
        y_off = jnp.dot(cg, hprev.astype(BF16), preferred_element_type=F32) * ea_x[:, g * gw:(g + 1) * gw]
        st_new = lax.dot_general(bg, xdec_b[:, g * gw:(g + 1) * gw], (((0,), (0,)), ((), ())),
                                 preferred_element_type=F32)
        st_ref[:, g * gw:(g + 1) * gw] = hprev * ea_x[q - 1:q, g * gw:(g + 1) * gw] + st_new
        y_parts.append(y_off)
    pp = heads_per_group // 2 + 1
    ys = []
    for g in range(SSD_GROUPS):
        diag = jnp.concatenate(y_parts[g * pp:g * pp + pp - 1], axis=1)
        ys.append(diag + y_parts[g * pp + pp - 1])
    y = jnp.concatenate(ys, axis=1) + xs * dskip_ref[...]
    o_ref[...] = _rms(y * _silu(z_ref[...]), ng_ref[...]).astype(o_ref.dtype)


def _ssd(xbc, z, dt, conv_w, conv_b, dt_bias, a_log, d_skip, norm_g, b, s, *, heads, hd, nstate):
    t, xbcw = xbc.shape
    width = heads * hd
    q = _tile(s, SSD_CHUNK, V7X_LANES)
    nc = s // q
    kconv = conv_w.shape[0]
    cw = jnp.zeros((V7X_SUBLANES, xbcw), F32).at[:kconv].set(conv_w)
    pad = lambda v: jnp.zeros((1, V7X_LANES), F32).at[0, :heads].set(v)
    tri = (jnp.arange(q)[None, :] <= jnp.arange(q)[:, None]).astype(BF16)
    hx = (jnp.arange(V7X_LANES)[:, None] == (jnp.arange(width)[None, :] // hd)).astype(BF16)
    row = lambda i, c: (i * nc + c, 0)
    fix = lambda i, c: (0, 0)
    blocks = (_nbytes((q, xbcw + width + V7X_LANES), F32) + _nbytes((q, width), BF16)
              + _nbytes((q, q), BF16) + _nbytes((V7X_LANES, width), BF16) + 8 * _nbytes((1, xbcw), F32))
    scratch = _nbytes((q + V7X_SUBLANES, xbcw), F32) + _nbytes((nstate, width), F32)
    temps = 24 * _nbytes((q, xbcw), F32)
    return pl.pallas_call(
        functools.partial(_ssd_kernel, q=q, width=width, nstate=nstate, hd=hd, kconv=kconv),
        grid=(b, nc),
        in_specs=[pl.BlockSpec((q, xbcw), row), pl.BlockSpec((q, width), row), pl.BlockSpec((q, V7X_LANES), row),
                  pl.BlockSpec((V7X_SUBLANES, xbcw), fix), pl.BlockSpec((1, xbcw), fix),
                  pl.BlockSpec((1, V7X_LANES), fix), pl.BlockSpec((1, V7X_LANES), fix),
                  pl.BlockSpec((1, width), fix), pl.BlockSpec((1, width), fix),
                  pl.BlockSpec((q, q), fix), pl.BlockSpec((V7X_LANES, width), fix)],
        out_specs=pl.BlockSpec((q, width), row),
        out_shape=jax.ShapeDtypeStruct((t, width), BF16),
        scratch_shapes=[pltpu.VMEM((q + V7X_SUBLANES, xbcw), F32), pltpu.VMEM((nstate, width), F32)],
        compiler_params=_params(("parallel", "arbitrary"), _vmem_limit(blocks, scratch, temps)),
        name="ssd",
    )(xbc, z, dt, cw, conv_b.reshape(1, xbcw), pad(dt_bias), pad(a_log),
      jnp.repeat(d_skip, hd).reshape(1, width), norm_g.reshape(1, width), tri, hx)


def _out_proj_kernel(x_ref, a_ref, y_ref, w_ref, o_ref, *, wa):
    acc = jnp.dot(a_ref[...], w_ref[0:wa, :], preferred_element_type=F32)
    acc = acc + jnp.dot(y_ref[...], w_ref[wa:, :], preferred_element_type=F32)
    o_ref[...] = x_ref[...] + acc


def _out_proj(x2, attn, ynorm, w):
    t, d = x2.shape
    wa, wy = attn.shape[1], ynorm.shape[1]
    tm = _tile(t, ROW_TILE)
    row = lambda i: (i, 0)
    blocks = 2 * _nbytes((tm, d), F32) + _nbytes((tm, wa + wy), BF16) + _nbytes(w.shape, BF16)
    return pl.pallas_call(
        functools.partial(_out_proj_kernel, wa=wa),
        grid=(t // tm,),
        in_specs=[pl.BlockSpec((tm, d), row), pl.BlockSpec((tm, wa), row), pl.BlockSpec((tm, wy), row),
                  pl.BlockSpec(w.shape, lambda i: (0, 0))],
        out_specs=pl.BlockSpec((tm, d), row),
        out_shape=jax.ShapeDtypeStruct((t, d), F32),
        compiler_params=_params(("parallel",), _vmem_limit(blocks, 0, 2 * _nbytes((tm, d), F32))),
        name="out_proj",
    )(x2, attn, ynorm, w)


def _ffn_kernel(h_ref, g_ref, w1_ref, w3_ref, w2_ref, o_ref, n_ref, acc_ref):
    f = pl.program_id(1)

    @pl.when(f == 0)
    def _():
        n_ref[...] = _rms(h_ref[...], g_ref[...]).astype(BF16)
        acc_ref[...] = jnp.zeros_like(acc_ref)

    n = n_ref[...]
    a = _silu(jnp.dot(n, w1_ref[...], preferred_element_type=F32)) * jnp.dot(n, w3_ref[...], preferred_element_type=F32)
    acc_ref[...] += jnp.dot(a.astype(BF16), w2_ref[...], preferred_element_type=F32)

    @pl.when(f == pl.num_programs(1) - 1)
    def _():
        o_ref[...] = h_ref[...] + acc_ref[...]


def _ffn(h, g, w1, w3, w2):
    t, d = h.shape
    ff = w1.shape[1]
    tm = _tile(t, FFN_ROW_TILE)
    tf = _tile(ff, FFN_COL_TILE, V7X_LANES)
    blocks = 2 * _nbytes((tm, d), F32) + 3 * _nbytes((d, tf), BF16)
    scratch = _nbytes((tm, d), BF16) + _nbytes((tm, d), F32)
    temps = 4 * _nbytes((tm, tf), F32) + _nbytes((tm, d), F32)
    return pl.pallas_call(
        _ffn_kernel,
        grid=(t // tm, ff // tf),
        in_specs=[pl.BlockSpec((tm, d), lambda i, f: (i, 0)), pl.BlockSpec((1, d), lambda i, f: (0, 0)),
                  pl.BlockSpec((d, tf), lambda i, f: (0, f)), pl.BlockSpec((d, tf), lambda i, f: (0, f)),
                  pl.BlockSpec((tf, d), lambda i, f: (f, 0))],
        out_specs=pl.BlockSpec((tm, d), lambda i, f: (i, 0)),
        out_shape=jax.ShapeDtypeStruct((t, d), F32),
        scratch_shapes=[pltpu.VMEM((tm, d), BF16), pltpu.VMEM((tm, d), F32)],
        compiler_params=_params(("parallel", "arbitrary"), _vmem_limit(blocks, scratch, temps)),
        name="ffn",
    )(h, g, w1, w3, w2)


def _ple_glu_kernel(h_ref, p_ref, g_ref, wg_ref, wp_ref, g2_ref, w_ref, b_ref, h_out_ref, u_ref, *, c):
    h = h_ref[...]
    n = _rms(h, g_ref[...]).astype(BF16)
    gate = _sigmoid(jnp.dot(n, wg_ref[...], preferred_element_type=F32))
    emb = jnp.dot(p_ref[...].astype(BF16), wp_ref[...], preferred_element_type=F32)
    h = h + emb * gate
    h_out_ref[...] = h
    n2 = _rms(h, g2_ref[...]).astype(BF16)
    val = jnp.dot(n2, w_ref[:, 0:c], preferred_element_type=F32) + b_ref[:, 0:c]
    glu_gate = jnp.dot(n2, w_ref[:, c:], preferred_element_type=F32) + b_ref[:, c:]
    u_ref[...] = val * _sigmoid(glu_gate)


def _ple_glu(h, p3, layer, g, w_gate, w_ple, g2, w, bias):
    t, d = h.shape
    pd = p3.shape[2]
    c = w.shape[1] // 2
    tm = _tile(t, ROW_TILE)
    row = lambda i: (i, 0)
    fix = lambda i: (0, 0)
    blocks = (2 * _nbytes((tm, d), F32) + _nbytes((tm, pd), F32) + _nbytes((tm, c), F32)
              + _nbytes((d + pd, d), BF16) + _nbytes(w.shape, BF16))
    return pl.pallas_call(
        functools.partial(_ple_glu_kernel, c=c),
        grid=(t // tm,),
        in_specs=[pl.BlockSpec((tm, d), row), pl.BlockSpec((None, tm, pd), lambda i: (layer, i, 0)),
                  pl.BlockSpec((1, d), fix), pl.BlockSpec((d, d), fix), pl.BlockSpec((pd, d), fix),
                  pl.BlockSpec((1, d), fix), pl.BlockSpec(w.shape, fix), pl.BlockSpec((1, 2 * c), fix)],
        out_specs=[pl.BlockSpec((tm, d), row), pl.BlockSpec((tm, c), row)],
        out_shape=[jax.ShapeDtypeStruct((t, d), F32), jax.ShapeDtypeStruct((t, c), F32)],
        compiler_params=_params(("parallel",), _vmem_limit(blocks, 0, 6 * _nbytes((tm, d), F32))),
        name="ple_glu",
    )(h, p3, g, w_gate, w_ple, g2, w, bias)


def _dwconv_kernel(u_ref, h_ref, dw_ref, db_ref, lg_ref, lb_ref, w_ref, b_ref, rg_ref, wr_ref, tri_ref,
                   o_ref, meta_ref, cnt_ref, ext_ref, sh_ref, cv_ref, run_ref, *, ts, kconv, rc, ne):
    halo = CONV_HALO
    sub = V7X_SUBLANES

    @pl.when((pl.program_id(0) == 0) & (pl.program_id(1) == 0))
    def _():
        run_ref[...] = jnp.zeros_like(run_ref)

    @pl.when(pl.program_id(1) == 0)
    def _():
        ext_ref[0:halo, :] = jnp.zeros((halo, ext_ref.shape[1]), F32)

    ext_ref[halo:halo + ts, :] = u_ref[...]
    base = halo - (kconv - 1)
    for r in range(1, sub):
        sh_ref[r - 1] = ext_ref[pl.ds(r, sh_ref.shape[1]), :]

    def tap(k, r0):
        a, r = divmod(base + k, sub)
        start = pl.multiple_of(r0 + a * sub, sub)
        if r == 0:
            return ext_ref[pl.ds(start, rc), :]
        return sh_ref[r - 1, pl.ds(start, rc), :]

    def chunk(i, c):
        r0 = pl.multiple_of(i * rc, rc)
        acc = db_ref[...] + dw_ref[0:1, :] * tap(0, r0)
        for k in range(1, kconv):
            acc = acc + dw_ref[k:k + 1, :] * tap(k, r0)
        cv_ref[pl.ds(r0, rc), :] = acc
        return c

    lax.fori_loop(0, ts // rc, chunk, 0)
    ext_ref[0:halo, :] = ext_ref[ts:ts + halo, :]

    cv = cv_ref[...]
    mu = jnp.mean(cv, axis=-1, keepdims=True)
    xc = cv - mu
    var = jnp.mean(xc * xc, axis=-1, keepdims=True)
    ln = xc * lax.rsqrt(var + EPS) * lg_ref[...] + lb_ref[...]
    act = _silu(ln).astype(BF16)
    h_new = h_ref[...] + jnp.dot(act, w_ref[...], preferred_element_type=F32) + b_ref[...]
    o_ref[...] = h_new
    _route_rows(h_new, rg_ref, wr_ref, tri_ref, run_ref, meta_ref, cnt_ref, ne)


def _dwconv_route(u, h, dw_w, dw_b, ln_g, ln_b, w2, b2, route_g, wr_pad, ne, b, s):
    t, c = u.shape
    d = h.shape[1]
    kconv = dw_w.shape[0]
    assert kconv - 1 <= CONV_HALO
    ts = _tile(s, CONV_ROW_TILE)
    rc = _tile(ts, CONV_ROW_CHUNK)
    nt = s // ts
    kp = -(-kconv // V7X_SUBLANES) * V7X_SUBLANES
    dwp = jnp.zeros((kp, c), F32).at[:kconv].set(dw_w)
    tri = (jnp.arange(ts)[None, :] < jnp.arange(ts)[:, None]).astype(BF16)
    row = lambda i, j: (i * nt + j, 0)
    fix = lambda i, j: (0, 0)
    blocks = (_nbytes((ts, c), F32) + 3 * _nbytes((ts, d), F32) + _nbytes((c, d), BF16) + _nbytes((kp, c), F32)
              + _nbytes((ts, V7X_LANES), F32) + _nbytes(wr_pad.shape, BF16) + _nbytes((ts, ts), BF16))
    sh_rows = ts + CONV_HALO - V7X_SUBLANES
    scratch = _nbytes((ts + CONV_HALO, c), F32) + _nbytes((ts, c), F32) + _nbytes((V7X_SUBLANES - 1, sh_rows, c), F32)
    return pl.pallas_call(
        functools.partial(_dwconv_kernel, ts=ts, kconv=kconv, rc=rc, ne=ne),
        grid=(b, nt),
        in_specs=[pl.BlockSpec((ts, c), row), pl.BlockSpec((ts, d), row), pl.BlockSpec((kp, c), fix),
                  pl.BlockSpec((1, c), fix), pl.BlockSpec((1, c), fix), pl.BlockSpec((1, c), fix),
                  pl.BlockSpec((c, d), fix), pl.BlockSpec((1, d), fix),
                  pl.BlockSpec((1, d), fix), pl.BlockSpec(wr_pad.shape, fix), pl.BlockSpec((ts, ts), fix)],
        out_specs=[pl.BlockSpec((ts, d), row), pl.BlockSpec((ts, V7X_LANES), row),
                   pl.BlockSpec((V7X_SUBLANES, V7X_LANES), fix)],
        out_shape=[jax.ShapeDtypeStruct((t, d), F32), jax.ShapeDtypeStruct((t, V7X_LANES), F32),
                   jax.ShapeDtypeStruct((V7X_SUBLANES, V7X_LANES), F32)],
        scratch_shapes=[pltpu.VMEM((ts + CONV_HALO, c), F32), pltpu.VMEM((V7X_SUBLANES - 1, sh_rows, c), F32),
                        pltpu.VMEM((ts, c), F32), pltpu.VMEM((1, V7X_LANES), F32)],
        compiler_params=_params(("arbitrary", "arbitrary"), _vmem_limit(blocks, scratch, 8 * _nbytes((ts, c), F32))),
        name="dwconv_route",
    )(u, h, dwp, dw_b.reshape(1, c), ln_g.reshape(1, c), ln_b.reshape(1, c), w2, b2.reshape(1, d),
      route_g, wr_pad, tri)


def _route_rows(h, g_ref, wr_ref, tri_ref, run_ref, meta_ref, cnt_ref, ne):
    nf = _rms(h, g_ref[...])
    logits = jnp.dot(nf.astype(BF16), wr_ref[...], preferred_element_type=F32)
    lane = lax.broadcasted_iota(jnp.int32, logits.shape, 1)
    neg = jnp.float32(-jnp.inf)
    logits = jnp.where(lane < ne, logits, neg)
    big = jnp.int32(V7X_LANES)
    m1 = jnp.max(logits, axis=-1, keepdims=True)
    i1 = jnp.min(jnp.where(logits == m1, lane, big), axis=-1, keepdims=True)
    sel1 = lane == i1
    rest = jnp.where(sel1, neg, logits)
    m2 = jnp.max(rest, axis=-1, keepdims=True)
    i2 = jnp.min(jnp.where(rest == m2, lane, big), axis=-1, keepdims=True)
    sel2 = lane == i2
    e2 = jnp.exp(m2 - m1)
    g1 = 1.0 / (1.0 + e2)
    g2 = e2 / (1.0 + e2)
    onehot = jnp.where(sel1 | sel2, 1.0, 0.0)
    before = jnp.dot(tri_ref[...], onehot.astype(BF16), preferred_element_type=F32) + run_ref[...]
    r1 = jnp.sum(jnp.where(sel1, before, 0.0), axis=-1, keepdims=True)
    r2 = jnp.sum(jnp.where(sel2, before, 0.0), axis=-1, keepdims=True)
    rows = onehot.shape[0]
    total = before[rows - 1:rows, :] + onehot[rows - 1:rows, :]
    run_ref[...] = total
    cnt_ref[...] = jnp.broadcast_to(total, cnt_ref.shape)
    meta = jnp.where(lane == 0, g1, jnp.where(lane == 1, g2, 0.0))
    meta = jnp.where(lane == 2, i1.astype(F32), jnp.where(lane == 3, i2.astype(F32), meta))
    meta = jnp.where(lane == 4, r1, jnp.where(lane == 5, r2, meta))
    meta_ref[...] = meta


def _moe_plan(meta, counts_f, tm, ne):
    t = meta.shape[0]
    nt = TOP_K * t // tm + ne
    e_sel = meta[:, 2:2 + TOP_K].astype(jnp.int32)
    rank = meta[:, 2 + TOP_K:2 + 2 * TOP_K].astype(jnp.int32)
    counts = counts_f[0, :ne].astype(jnp.int32)
    padded = ((counts + tm - 1) // tm) * tm
    gend = jnp.cumsum(padded)
    gstart = gend - padded
    slot = gstart[e_sel] + rank
    tile_start = jnp.arange(nt, dtype=jnp.int32) * tm
    tile_e = jnp.sum((tile_start[:, None] >= gend[None, :]).astype(jnp.int32), axis=1)
    valid = (tile_e < ne).astype(jnp.int32)
    return slot, jnp.minimum(tile_e, ne - 1), valid, nt * tm


def _sc_geometry(t, d, dtype):
    sc = pltpu.get_tpu_info().sparse_core
    nc, nw = sc.num_cores, sc.num_cores * sc.num_subcores
    per_w = t // nw
    chunk = _tile(per_w, SC_GATHER_ROWS)
    assert t % nw == 0 and per_w % chunk == 0 and chunk % V7X_SUBLANES == 0
    assert TOP_K * chunk * (d * jnp.dtype(dtype).itemsize + 4) <= sc.vmem_capacity_bytes
    mesh = plsc.VectorSubcoreMesh(core_axis_name="c", subcore_axis_name="s")
    return nc, per_w, chunk, mesh


def _dispatch_rows(rows, slot0, slot1, n_slots):
    t, d = rows.shape
    nc, per_w, chunk, mesh = _sc_geometry(t, d, rows.dtype)

    @functools.partial(
        pl.kernel, mesh=mesh, out_type=jax.ShapeDtypeStruct((n_slots, d), rows.dtype),
        scratch_types=[pltpu.VMEM((chunk,), jnp.int32), pltpu.VMEM((chunk,), jnp.int32),
                       pltpu.VMEM((chunk, d), rows.dtype)])
    def dispatch(rows_hbm, s0_hbm, s1_hbm, out_hbm, i0_v, i1_v, rows_v):
        base = (lax.axis_index("s") * nc + lax.axis_index("c")) * per_w

        @pl.loop(0, per_w // chunk)
        def _(c):
            off = pl.multiple_of(base + c * chunk, chunk)
            pltpu.sync_copy(s0_hbm.at[pl.ds(off, chunk)], i0_v)
            pltpu.sync_copy(s1_hbm.at[pl.ds(off, chunk)], i1_v)
            pltpu.sync_copy(rows_hbm.at[pl.ds(off, chunk)], rows_v)
            pltpu.sync_copy(rows_v, out_hbm.at[i0_v])
            pltpu.sync_copy(rows_v, out_hbm.at[i1_v])

    return dispatch(rows, slot0, slot1)


def _combine_rows(table, slot0, slot1):
    t = slot0.shape[0]
    d = table.shape[1]
    nc, per_w, chunk, mesh = _sc_geometry(t, d, table.dtype)
    out = jax.ShapeDtypeStruct((t, d), table.dtype)

    @functools.partial(
        pl.kernel, mesh=mesh, out_type=(out, out),
        scratch_types=[pltpu.VMEM((chunk,), jnp.int32), pltpu.VMEM((chunk,), jnp.int32),
                       pltpu.VMEM((chunk, d), table.dtype), pltpu.VMEM((chunk, d), table.dtype),
                       pltpu.SemaphoreType.DMA, pltpu.SemaphoreType.DMA])
    def combine(table_hbm, s0_hbm, s1_hbm, y0_hbm, y1_hbm, i0_v, i1_v, r0_v, r1_v, sem0, sem1):
        base = (lax.axis_index("s") * nc + lax.axis_index("c")) * per_w

        @pl.loop(0, per_w // chunk)
        def _(c):
            off = pl.multiple_of(base + c * chunk, chunk)
            pltpu.sync_copy(s0_hbm.at[pl.ds(off, chunk)], i0_v)
            pltpu.sync_copy(s1_hbm.at[pl.ds(off, chunk)], i1_v)
            g0 = pltpu.async_copy(table_hbm.at[i0_v], r0_v, sem0)
            g1 = pltpu.async_copy(table_hbm.at[i1_v], r1_v, sem1)
            g0.wait()
            pltpu.sync_copy(r0_v, y0_hbm.at[pl.ds(off, chunk)])
            g1.wait()
            pltpu.sync_copy(r1_v, y1_hbm.at[pl.ds(off, chunk)])

    return combine(table, slot0, slot1)


def _moe_kernel(te_ref, tv_ref, x_ref, g_ref, w1_ref, w3_ref, w2_ref, o_ref, n_ref, acc_ref):
    i = pl.program_id(0)
    f = pl.program_id(1)
    last = pl.num_programs(1) - 1
    valid = tv_ref[i] == 1

    @pl.when(valid & (f == 0))
    def _():
        n_ref[...] = _rms(x_ref[...], g_ref[...]).astype(BF16)
        acc_ref[...] = jnp.zeros_like(acc_ref)

    @pl.when(valid)
    def _():
        n = n_ref[...]
        w1 = w1_ref[0].astype(BF16)
        w3 = w3_ref[0].astype(BF16)
        w2 = w2_ref[0].astype(BF16)
        a = _silu(jnp.dot(n, w1, preferred_element_type=F32)) * jnp.dot(n, w3, preferred_element_type=F32)
        acc_ref[...] += jnp.dot(a.astype(BF16), w2, preferred_element_type=F32)

    @pl.when(valid & (f == last))
    def _():
        o_ref[...] = acc_ref[...]

    @pl.when(jnp.logical_not(valid) & (f == last))
    def _():
        o_ref[...] = jnp.zeros_like(o_ref)


def _moe_experts(xs, g, tile_e, tile_valid, w1, w3, w2, tm):
    ns, d = xs.shape
    ne, _, ff = w1.shape
    tf = _tile(ff, MOE_COL_TILE, V7X_LANES)
    nf = ff // tf
    fidx = lambda f, tv, i: f * tv[i] + (nf - 1) * (1 - tv[i])
    blocks = 2 * _nbytes((tm, d), F32) + 3 * _nbytes((d, tf), w1.dtype)
    scratch = _nbytes((tm, d), BF16) + _nbytes((tm, d), F32)
    temps = 4 * _nbytes((tm, tf), F32) + _nbytes((tm, d), F32) + 3 * _nbytes((d, tf), BF16)
    grid_spec = pltpu.PrefetchScalarGridSpec(
        num_scalar_prefetch=2,
        grid=(ns // tm, nf),
        in_specs=[pl.BlockSpec((tm, d), lambda i, f, te, tv: (i, 0)),
                  pl.BlockSpec((1, d), lambda i, f, te, tv: (0, 0)),
                  pl.BlockSpec((1, d, tf), lambda i, f, te, tv: (te[i], 0, fidx(f, tv, i))),
                  pl.BlockSpec((1, d, tf), lambda i, f, te, tv: (te[i], 0, fidx(f, tv, i))),
                  pl.BlockSpec((1, tf, d), lambda i, f, te, tv: (te[i], fidx(f, tv, i), 0))],
        out_specs=pl.BlockSpec((tm, d), lambda i, f, te, tv: (i, 0)),
        scratch_shapes=[pltpu.VMEM((tm, d), BF16), pltpu.VMEM((tm, d), F32)])
    return pl.pallas_call(
        _moe_kernel,
        grid_spec=grid_spec,
        out_shape=jax.ShapeDtypeStruct((ns, d), F32),
        compiler_params=_params(("arbitrary", "arbitrary"), _vmem_limit(blocks, scratch, temps)),
        name="moe_experts",
    )(tile_e, tile_valid, xs, g, w1, w3, w2)


def _combine_ple_kernel(h_ref, y0_ref, y1_ref, m_ref, p_ref, g_ref, wg_ref, wp_ref, fg_ref, o_ref):
    meta = m_ref[...]
    h = h_ref[...] + meta[:, 0:1] * y0_ref[...] + meta[:, 1:2] * y1_ref[...]
    n = _rms(h, g_ref[...]).astype(BF16)
    gate = _sigmoid(jnp.dot(n, wg_ref[...], preferred_element_type=F32))
    emb = jnp.dot(p_ref[...].astype(BF16), wp_ref[...], preferred_element_type=F32)
    o_ref[...] = _rms(h + emb * gate, fg_ref[...])


def _combine_ple(h, y0, y1, meta, p3, layer, g, w_gate, w_ple, final_g):
    t, d = h.shape
    pd = p3.shape[2]
    tm = _tile(t, ROW_TILE)
    row = lambda i: (i, 0)
    fix = lambda i: (0, 0)
    blocks = (4 * _nbytes((tm, d), F32) + _nbytes((tm, pd + V7X_LANES), F32) + _nbytes((d + pd, d), BF16))
    return pl.pallas_call(
        _combine_ple_kernel,
        grid=(t // tm,),
        in_specs=[pl.BlockSpec((tm, d), row), pl.BlockSpec((tm, d), row), pl.BlockSpec((tm, d), row),
                  pl.BlockSpec((tm, V7X_LANES), row),
                  pl.BlockSpec((None, tm, pd), lambda i: (layer, i, 0)), pl.BlockSpec((1, d), fix),
                  pl.BlockSpec((d, d), fix), pl.BlockSpec((pd, d), fix), pl.BlockSpec((1, d), fix)],
        out_specs=pl.BlockSpec((tm, d), row),
        out_shape=jax.ShapeDtypeStruct((t, d), F32),
        compiler_params=_params(("parallel",), _vmem_limit(blocks, 0, 4 * _nbytes((tm, d), F32))),
        name="combine_ple_final",
    )(h, y0, y1, meta, p3, g, w_gate, w_ple, final_g)


def kernel(x, p, mix_norm_g, ffn_norm_g, ple_norm_g, w_ple, w_ple_gate, ev_w_in, ev_conv_w, ev_conv_b, ev_dt_bias, ev_a_log, ev_d, ev_ssd_norm_g, ev_w_out, ev_ffn_w1, ev_ffn_w3, ev_ffn_w2, od_pw1_w, od_pw1_b, od_dw_w, od_dw_b, od_ln_g, od_ln_b, od_pw2_w, od_pw2_b, od_router_w, od_moe_w1, od_moe_w3, od_moe_w2, final_norm_g):
    b, s, d = x.shape
    t = b * s
    depth = p.shape[0]
    assert depth == 2 and ev_w_in.shape[0] == 1 and od_pw1_w.shape[0] == 1
    sbw = SB_HEADS * SB_HEAD_DIM
    ssdw = SSD_HEADS * SSD_HEAD_DIM
    xbcw = ssdw + 2 * SSD_GROUPS * SSD_STATE
    in_width = 3 * sbw + ssdw + xbcw + SSD_HEADS
    assert ev_w_in.shape[2] == in_width
    bf = lambda w: w.astype(BF16)
    vec = lambda v: v.reshape(1, -1)

    x2 = x.reshape(t, d)
    p3 = p.reshape(depth, t, -1)

    w_all = bf(ev_w_in[0])
    w_rest = w_all[:, 3 * sbw:]
    w_in = jnp.concatenate([w_all[:, :2 * sbw], w_rest,
                            jnp.zeros((d, V7X_LANES - SSD_HEADS), BF16)], axis=1)
    w_vt = w_all[:, 2 * sbw:3 * sbw].T
    qscale = LOG2E / math.sqrt(SB_HEAD_DIM)
    qk, vt, z, xbc, dt = _in_proj(x2, vec(mix_norm_g[0]), w_in, w_vt, sbw=sbw, zw=ssdw, xbcw=xbcw, qscale=qscale)
    attn = _sb_attention(qk, vt, b, s, heads=SB_HEADS, hd=SB_HEAD_DIM)
    ynorm = _ssd(xbc, z, dt, ev_conv_w[0], ev_conv_b[0], ev_dt_bias[0], ev_a_log[0], ev_d[0], ev_ssd_norm_g[0],
                 b, s, heads=SSD_HEADS, hd=SSD_HEAD_DIM, nstate=SSD_STATE)
    h = _out_proj(x2, attn, ynorm, bf(ev_w_out[0]))
    h = _ffn(h, vec(ffn_norm_g[0]), bf(ev_ffn_w1[0]), bf(ev_ffn_w3[0]), bf(ev_ffn_w2[0]))
    h, u = _ple_glu(h, p3, 0, vec(ple_norm_g[0]), bf(w_ple_gate[0]), bf(w_ple[0]),
                    vec(mix_norm_g[1]), bf(od_pw1_w[0]), vec(od_pw1_b[0]))
    ne = od_router_w.shape[2]
    wr = jnp.zeros((d, V7X_LANES), BF16).at[:, :ne].set(bf(od_router_w[0]))
    h, meta, counts = _dwconv_route(u, h, od_dw_w[0], od_dw_b[0], od_ln_g[0], od_ln_b[0], bf(od_pw2_w[0]),
                                       od_pw2_b[0], vec(ffn_norm_g[1]), wr, ne, b, s)
    tm = _tile(TOP_K * t, FFN_ROW_TILE)
    slot, tile_e, tile_valid, n_slots = _moe_plan(meta, counts, tm, ne)
    xs = _dispatch_rows(h, slot[:, 0], slot[:, 1], n_slots)
    ys = _moe_experts(xs, vec(ffn_norm_g[1]), tile_e, tile_valid, od_moe_w1[0], od_moe_w3[0], od_moe_w2[0], tm)
    y0, y1 = _combine_rows(ys, slot[:, 0], slot[:, 1])
    out = _combine_ple(h, y0, y1, meta, p3, 1, vec(ple_norm_g[1]), bf(w_ple_gate[1]), bf(w_ple[1]), vec(final_norm_g))
    return out.reshape(b, s, d)
```

```python
import functools
import math

import jax
import jax.numpy as jnp
from jax import lax
from jax.experimental import pallas as pl
from jax.experimental.pallas import tpu as pltpu
from jax.experimental.pallas import tpu_sc as plsc

F32 = jnp.float32
BF16 = jnp.bfloat16
EPS = 1e-6
LOG2E = 1.4426950408889634

V7X_LANES = 128
V7X_SUBLANES = 8
V7X_VMEM_BYTES = 64 * 1024 * 1024

SB_HEADS = 8
SB_HEAD_DIM = 64
SSD_HEADS = 8
SSD_HEAD_DIM = 64
SSD_STATE = 128
SSD_GROUPS = 2
SSD_CONV = 4
N_EXPERTS = 8
TOP_K = 2

ROW_TILE = 1024
FFN_ROW_TILE = 1024
FFN_COL_TILE = 1408
MOE_COL_TILE = 512
ATTN_BLOCK = 256
ATTN_PAIRS = 2
SSD_CHUNK = 128
CONV_ROW_TILE = 512
CONV_ROW_CHUNK = 32
CONV_HALO = 32
SC_GATHER_ROWS = 32


def _tile(n, pref, mult=V7X_SUBLANES):
    if n <= pref:
        return n
    t = (pref // mult) * mult
    while t > mult and n % t:
        t -= mult
    assert n % t == 0, (n, pref, mult)
    return t


def _vmem_limit(block_bytes, scratch_bytes=0, temp_bytes=0):
    need = 2 * block_bytes + scratch_bytes + temp_bytes + (4 << 20)
    return int(min(max(need, 16 << 20), V7X_VMEM_BYTES - (6 << 20)))


def _params(sem, vmem):
    return pltpu.CompilerParams(dimension_semantics=sem, vmem_limit_bytes=vmem)


def _nbytes(shape, dtype):
    return math.prod(shape) * jnp.dtype(dtype).itemsize


def _rms(x, g):
    return x * lax.rsqrt(jnp.mean(x * x, axis=-1, keepdims=True) + EPS) * g


def _sigmoid(x):
    return 1.0 / (1.0 + jnp.exp(-x))


def _silu(x):
    return x * _sigmoid(x)


def _softplus(x):
    return jnp.maximum(x, 0.0) + jnp.log(1.0 + jnp.exp(-jnp.abs(x)))


def _split3(x):
    hi = x.astype(BF16)
    r1 = x - hi.astype(F32)
    mid = r1.astype(BF16)
    lo = (r1 - mid.astype(F32)).astype(BF16)
    return hi, mid, lo


def _dot_exact_rhs(x, m):
    hi, mid, lo = _split3(x)
    d = functools.partial(jnp.dot, preferred_element_type=F32)
    return d(hi, m) + d(mid, m) + d(lo, m)


def _dot_exact_lhs(m, x):
    hi, mid, lo = _split3(x)
    d = functools.partial(jnp.dot, preferred_element_type=F32)
    return d(m, hi) + d(m, mid) + d(m, lo)


def _in_proj_kernel(x_ref, g_ref, w_ref, wvt_ref, qk_ref, vt_ref, z_ref, xbc_ref, dt_ref, *, sbw, zw, xbcw, qscale):
    hn = _rms(x_ref[...], g_ref[...]).astype(BF16)

    def mm(lo, hi):
        return jnp.dot(hn, w_ref[:, lo:hi], preferred_element_type=F32)

    qk_ref[:, 0:sbw] = (mm(0, sbw) * qscale).astype(BF16)
    qk_ref[:, sbw:2 * sbw] = mm(sbw, 2 * sbw).astype(BF16)
    vt_ref[...] = lax.dot_general(wvt_ref[...], hn, (((1,), (1,)), ((), ())),
                                  preferred_element_type=F32).astype(BF16)
    o = 2 * sbw
    z_ref[...] = mm(o, o + zw)
    o += zw
    xbc_ref[...] = mm(o, o + xbcw)
    o += xbcw
    dt_ref[...] = mm(o, o + V7X_LANES)


def _in_proj(x2, g, w_pad, w_vt, *, sbw, zw, xbcw, qscale):
    t, d = x2.shape
    n = w_pad.shape[1]
    tm = _tile(t, ROW_TILE, V7X_LANES)
    blocks = (_nbytes((tm, d), F32) + _nbytes((d, n + sbw), BF16) + _nbytes((tm, 3 * sbw), BF16)
              + _nbytes((tm, zw + xbcw + V7X_LANES), F32))
    return pl.pallas_call(
        functools.partial(_in_proj_kernel, sbw=sbw, zw=zw, xbcw=xbcw, qscale=qscale),
        grid=(t // tm,),
        in_specs=[pl.BlockSpec((tm, d), lambda i: (i, 0)),
                  pl.BlockSpec((1, d), lambda i: (0, 0)),
                  pl.BlockSpec((d, n), lambda i: (0, 0)),
                  pl.BlockSpec((sbw, d), lambda i: (0, 0))],
        out_specs=[pl.BlockSpec((tm, 2 * sbw), lambda i: (i, 0)),
                   pl.BlockSpec((sbw, tm), lambda i: (0, i)),
                   pl.BlockSpec((tm, zw), lambda i: (i, 0)),
                   pl.BlockSpec((tm, xbcw), lambda i: (i, 0)),
                   pl.BlockSpec((tm, V7X_LANES), lambda i: (i, 0))],
        out_shape=[jax.ShapeDtypeStruct((t, 2 * sbw), BF16),
                   jax.ShapeDtypeStruct((sbw, t), BF16),
                   jax.ShapeDtypeStruct((t, zw), F32),
                   jax.ShapeDtypeStruct((t, xbcw), F32),
                   jax.ShapeDtypeStruct((t, V7X_LANES), F32)],
        compiler_params=_params(("parallel",), _vmem_limit(blocks, 0, _nbytes((tm, n), F32))),
        name="in_proj",
    )(x2, g, w_pad, w_vt)


ATTN_MASKED = -1e30


def _sb_attn_kernel(q_ref, k_ref, vt_ref, m_ref, o_ref, *scratch, blk, hd, npair):
    qi = pl.program_id(2)
    nblocks = qi + 1
    width = 2 * hd
    nh = 2 * npair
    acc_refs, car_refs, z_refs, incl_refs = (scratch[i * nh:(i + 1) * nh] for i in range(4))
    lane = lax.broadcasted_iota(jnp.int32, (1, width), 1)
    qh = []
    for p in range(npair):
        q = q_ref[:, p * width:(p + 1) * width]
        qz = jnp.zeros_like(q)
        qh += [jnp.where(lane < hd, q, qz), jnp.where(lane >= hd, q, qz)]
    sign = jnp.uint32(0x80000000)

    def key_start(m):
        j = jnp.where(m < nblocks, qi - m, qi)
        return pl.multiple_of(j * blk, blk)

    def scores(m, h):
        p = h // 2
        kblk = k_ref[pl.ds(key_start(m), blk), p * width:(p + 1) * width]
        return lax.dot_general(kblk, qh[h], (((1,), (1,)), ((), ())), preferred_element_type=F32)

    def cumsum_stage(h, z_cur, i_cur):
        nabs = pltpu.bitcast(pltpu.bitcast(z_cur, jnp.uint32) | sign, F32)
        sp2 = jnp.maximum(z_cur, 0.0) + jnp.log2(1.0 + jnp.exp2(nabs))
        incl_refs[h][i_cur] = jnp.dot(m_ref[...], sp2.astype(BF16), preferred_element_type=F32)

    def weight_stage(h, z_prev, incl, vstart):
        p = h // 2
        car = car_refs[h][...]
        w = jnp.exp2(z_prev - incl - car)
        vtblk = vt_ref[p * width:(p + 1) * width, pl.ds(vstart, blk)]
        acc_refs[h][...] += jnp.dot(vtblk, w.astype(BF16), preferred_element_type=F32)
        car_refs[h][...] = car + incl[0:1, :]

    row = lax.broadcasted_iota(jnp.int32, (blk, blk), 0)
    col = lax.broadcasted_iota(jnp.int32, (blk, blk), 1)
    for h in range(nh):
        acc_refs[h][...] = jnp.zeros_like(acc_refs[h])
        car_refs[h][...] = jnp.zeros_like(car_refs[h])
        z_refs[h][0] = jnp.where(row < col, scores(0, h), ATTN_MASKED)

    def first(m, c):
        for h in range(nh):
            cumsum_stage(h, z_refs[h][0], 0)
            z_refs[h][1] = scores(1, h)
        return c

    lax.fori_loop(0, jnp.minimum(nblocks, 1), first, 0)

    def body(m, c):
        s_next = lax.rem(m + 1, 3)
        s_cur = lax.rem(m, 3)
        s_prev = lax.rem(m + 2, 3)
        i_cur = lax.rem(m, 2)
        i_prev = lax.rem(m + 1, 2)
        vstart = key_start(m - 1)
        for h in range(nh):
            z_prev = z_refs[h][s_prev]
            incl = incl_refs[h][i_prev]
            z_cur = z_refs[h][s_cur]
            weight_stage(h, z_prev, incl, vstart)
            cumsum_stage(h, z_cur, i_cur)
            z_refs[h][s_next] = scores(m + 1, h)
        return c

    lax.fori_loop(1, nblocks, body, 0)
    last = nblocks - 1
    for h in range(nh):
        weight_stage(h, z_refs[h][lax.rem(last, 3)], incl_refs[h][lax.rem(last, 2)], key_start(last))
    srow = lax.broadcasted_iota(jnp.int32, (width, 1), 0)
    for p in range(npair):
        both = jnp.where(srow < hd, acc_refs[2 * p][...], acc_refs[2 * p + 1][...])
        o_ref[:, p * width:(p + 1) * width] = both.T.astype(o_ref.dtype)


def _sb_attention(qk, vt, b, s, *, heads, hd):
    t = qk.shape[0]
    blk = _tile(s, ATTN_BLOCK, V7X_LANES)
    nq = s // blk
    width = 2 * hd
    assert width == V7X_LANES and heads % (2 * ATTN_PAIRS) == 0
    npair = ATTN_PAIRS
    gw = npair * width
    ng = heads * hd // gw
    m_t = (jnp.arange(blk)[None, :] >= jnp.arange(blk)[:, None]).astype(BF16)
    blocks = 2 * _nbytes((blk, gw), BF16) + 2 * _nbytes((s, gw), BF16) + _nbytes((blk, blk), BF16)
    nh = 2 * npair
    scratch = nh * (_nbytes((width, blk), F32) + _nbytes((V7X_SUBLANES, blk), F32) + 5 * _nbytes((blk, blk), F32))
    temps = 4 * nh * _nbytes((blk, blk), F32)
    return pl.pallas_call(
        functools.partial(_sb_attn_kernel, blk=blk, hd=hd, npair=npair),
        grid=(b, ng, nq),
        in_specs=[pl.BlockSpec((blk, gw), lambda bi, g, qi: (bi * nq + qi, g)),
                  pl.BlockSpec((s, gw), lambda bi, g, qi: (bi, ng + g)),
                  pl.BlockSpec((gw, s), lambda bi, g, qi: (g, bi)),
                  pl.BlockSpec((blk, blk), lambda bi, g, qi: (0, 0))],
        out_specs=pl.BlockSpec((blk, gw), lambda bi, g, qi: (bi * nq + qi, g)),
        out_shape=jax.ShapeDtypeStruct((t, heads * hd), BF16),
        scratch_shapes=([pltpu.VMEM((width, blk), F32)] * nh + [pltpu.VMEM((1, blk), F32)] * nh
                        + [pltpu.VMEM((3, blk, blk), F32)] * nh + [pltpu.VMEM((2, blk, blk), F32)] * nh),
        compiler_params=_params(("parallel", "parallel", "arbitrary"), _vmem_limit(blocks, scratch, temps)),
        name="sb_attention",
    )(qk, qk, vt, m_t)


def _ssd_kernel(xbc_ref, z_ref, dt_ref, cw_ref, cb_ref, dtb_ref, alog_ref, dskip_ref, ng_ref, tri_ref, exp_ref,
                o_ref, ext_ref, st_ref, *, q, width, nstate, hd, kconv):
    halo = V7X_SUBLANES
    gw = width // SSD_GROUPS

    @pl.when(pl.program_id(1) == 0)
    def _():
        ext_ref[0:halo, :] = jnp.zeros((halo, ext_ref.shape[1]), F32)
        st_ref[...] = jnp.zeros_like(st_ref)

    x_new = xbc_ref[...]
    ext_ref[halo:halo + q, :] = x_new
    conv = cb_ref[...] + cw_ref[0:1, :] * ext_ref[pl.ds(halo - (kconv - 1), q), :]
    for k in range(1, kconv):
        conv = conv + cw_ref[k:k + 1, :] * ext_ref[pl.ds(halo - (kconv - 1) + k, q), :]
    ext_ref[0:halo, :] = x_new[q - halo:q, :]
    xbc = _silu(conv)
    xs = xbc[:, :width]
    bm = xbc[:, width:width + SSD_GROUPS * nstate]
    cm = xbc[:, width + SSD_GROUPS * nstate:]

    dt = _softplus(dt_ref[...] + dtb_ref[...])
    a = -jnp.exp(alog_ref[...])
    acum = _dot_exact_lhs(tri_ref[...], dt * a)
    ea = jnp.exp(acum)
    dte = jnp.exp(acum[q - 1:q, :] - acum)
    ex = exp_ref[...]
    dt_x = _dot_exact_rhs(dt, ex)
    ea_x = _dot_exact_rhs(ea, ex)
    dte_x = _dot_exact_rhs(dte, ex)
    xdt = xs * dt_x
    xdt_b = xdt.astype(BF16)
    xdec_b = (xdt * dte_x).astype(BF16)
    acum_t = acum.T

    row = lax.broadcasted_iota(jnp.int32, (q, q), 0)
    col = lax.broadcasted_iota(jnp.int32, (q, q), 1)
    causal = col <= row
    lane = lax.broadcasted_iota(jnp.int32, (1, 2 * hd), 1)
    heads_per_group = gw // hd
    y_parts = []
    for g in range(SSD_GROUPS):
        bg = bm[:, g * nstate:(g + 1) * nstate].astype(BF16)
        cg = cm[:, g * nstate:(g + 1) * nstate].astype(BF16)
        gmat = lax.dot_general(cg, bg, (((1,), (1,)), ((), ())), preferred_element_type=F32)
        for pr in range(heads_per_group // 2):
            c0 = g * gw + pr * 2 * hd
            xpair = xdt_b[:, c0:c0 + 2 * hd]
            yd = []
            for hh in range(2):
                h = (c0 // hd) + hh
                seg = acum[:, h:h + 1] - acum_t[h:h + 1, :]
                lmat = jnp.exp(jnp.where(causal, seg, -jnp.inf))
                yd.append(jnp.dot((gmat * lmat).astype(BF16), xpair, preferred_element_type=F32))
            y_parts.append(jnp.where(lane < hd, yd[0], yd[1]))
        hprev = st_ref[:, g * gw:(g + 1) * gw]
        y_off = jnp.dot(cg, hprev.astype(BF16), preferred_element_type=F32) * ea_x[:, g * gw:(g + 1) * gw]
        st_new = lax.dot_general(bg, xdec_b[:, g * gw:(g + 1) * gw], (((0,), (0,)), ((), ())),
                                 preferred_element_type=F32)
        st_ref[:, g * gw:(g + 1) * gw] = hprev * ea_x[q - 1:q, g * gw:(g + 1) * gw] + st_new
        y_parts.append(y_off)
    pp = heads_per_group // 2 + 1
    ys = []
    for g in range(SSD_GROUPS):
        diag = jnp.concatenate(y_parts[g * pp:g * pp + pp - 1], axis=1)
        ys.append(diag + y_parts[g * pp + pp - 1])
    y = jnp.concatenate(ys, axis=1) + xs * dskip_ref[...]
    o_ref[...] = _rms(y * _silu(z_ref[...]), ng_ref[...]).astype(o_ref.dtype)


def _ssd(xbc, z, dt, conv_w, conv_b, dt_bias, a_log, d_skip, norm_g, b, s, *, heads, hd, nstate):
    t, xbcw = xbc.shape
    width = heads * hd
    q = _tile(s, SSD_CHUNK, V7X_LANES)
    nc = s // q
    kconv = conv_w.shape[0]
    cw = jnp.zeros((V7X_SUBLANES, xbcw), F32).at[:kconv].set(conv_w)
    pad = lambda v: jnp.zeros((1, V7X_LANES), F32).at[0, :heads].set(v)
    tri = (jnp.arange(q)[None, :] <= jnp.arange(q)[:, None]).astype(BF16)
    hx = (jnp.arange(V7X_LANES)[:, None] == (jnp.arange(width)[None, :] // hd)).astype(BF16)
    row = lambda i, c: (i * nc + c, 0)
    fix = lambda i, c: (0, 0)
    blocks = (_nbytes((q, xbcw + width + V7X_LANES), F32) + _nbytes((q, width), BF16)
              + _nbytes((q, q), BF16) + _nbytes((V7X_LANES, width), BF16) + 8 * _nbytes((1, xbcw), F32))
    scratch = _nbytes((q + V7X_SUBLANES, xbcw), F32) + _nbytes((nstate, width), F32)
    temps = 24 * _nbytes((q, xbcw), F32)
    return pl.pallas_call(
        functools.partial(_ssd_kernel, q=q, width=width, nstate=nstate, hd=hd, kconv=kconv),
        grid=(b, nc),
        in_specs=[pl.BlockSpec((q, xbcw), row), pl.BlockSpec((q, width), row), pl.BlockSpec((q, V7X_LANES), row),
                  pl.BlockSpec((V7X_SUBLANES, xbcw), fix), pl.BlockSpec((1, xbcw), fix),
                  pl.BlockSpec((1, V7X_LANES), fix), pl.BlockSpec((1, V7X_LANES), fix),
                  pl.BlockSpec((1, width), fix), pl.BlockSpec((1, width), fix),
                  pl.BlockSpec((q, q), fix), pl.BlockSpec((V7X_LANES, width), fix)],
        out_specs=pl.BlockSpec((q, width), row),
        out_shape=jax.ShapeDtypeStruct((t, width), BF16),
        scratch_shapes=[pltpu.VMEM((q + V7X_SUBLANES, xbcw), F32), pltpu.VMEM((nstate, width), F32)],
        compiler_params=_params(("parallel", "arbitrary"), _vmem_limit(blocks, scratch, temps)),
        name="ssd",
    )(xbc, z, dt, cw, conv_b.reshape(1, xbcw), pad(dt_bias), pad(a_log),
      jnp.repeat(d_skip, hd).reshape(1, width), norm_g.reshape(1, width), tri, hx)


def _out_proj_kernel(x_ref, a_ref, y_ref, w_ref, o_ref, *, wa):
    acc = jnp.dot(a_ref[...], w_ref[0:wa, :], preferred_element_type=F32)
    acc = acc + jnp.dot(y_ref[...], w_ref[wa:, :], preferred_element_type=F32)
    o_ref[...] = x_ref[...] + acc


def _out_proj(x2, attn, ynorm, w):
    t, d = x2.shape
    wa, wy = attn.shape[1], ynorm.shape[1]
    tm = _tile(t, ROW_TILE)
    row = lambda i: (i, 0)
    blocks = 2 * _nbytes((tm, d), F32) + _nbytes((tm, wa + wy), BF16) + _nbytes(w.shape, BF16)
    return pl.pallas_call(
        functools.partial(_out_proj_kernel, wa=wa),
        grid=(t // tm,),
        in_specs=[pl.BlockSpec((tm, d), row), pl.BlockSpec((tm, wa), row), pl.BlockSpec((tm, wy), row),
                  pl.BlockSpec(w.shape, lambda i: (0, 0))],
        out_specs=pl.BlockSpec((tm, d), row),
        out_shape=jax.ShapeDtypeStruct((t, d), F32),
        compiler_params=_params(("parallel",), _vmem_limit(blocks, 0, 2 * _nbytes((tm, d), F32))),
        name="out_proj",
    )(x2, attn, ynorm, w)


def _ffn_kernel(h_ref, g_ref, w1_ref, w3_ref, w2_ref, o_ref, n_ref, acc_ref):
    f = pl.program_id(1)

    @pl.when(f == 0)
    def _():
        n_ref[...] = _rms(h_ref[...], g_ref[...]).astype(BF16)
        acc_ref[...] = jnp.zeros_like(acc_ref)

    n = n_ref[...]
    a = _silu(jnp.dot(n, w1_ref[...], preferred_element_type=F32)) * jnp.dot(n, w3_ref[...], preferred_element_type=F32)
    acc_ref[...] += jnp.dot(a.astype(BF16), w2_ref[...], preferred_element_type=F32)

    @pl.when(f == pl.num_programs(1) - 1)
    def _():
        o_ref[...] = h_ref[...] + acc_ref[...]


def _ffn(h, g, w1, w3, w2):
    t, d = h.shape
    ff = w1.shape[1]
    tm = _tile(t, FFN_ROW_TILE)
    tf = _tile(ff, FFN_COL_TILE, V7X_LANES)
    blocks = 2 * _nbytes((tm, d), F32) + 3 * _nbytes((d, tf), BF16)
    scratch = _nbytes((tm, d), BF16) + _nbytes((tm, d), F32)
    temps = 4 * _nbytes((tm, tf), F32) + _nbytes((tm, d), F32)
    return pl.pallas_call(
        _ffn_kernel,
        grid=(t // tm, ff // tf),
        in_specs=[pl.BlockSpec((tm, d), lambda i, f: (i, 0)), pl.BlockSpec((1, d), lambda i, f: (0, 0)),
                  pl.BlockSpec((d, tf), lambda i, f: (0, f)), pl.BlockSpec((d, tf), lambda i, f: (0, f)),
                  pl.BlockSpec((tf, d), lambda i, f: (f, 0))],
        out_specs=pl.BlockSpec((tm, d), lambda i, f: (i, 0)),
        out_shape=jax.ShapeDtypeStruct((t, d), F32),
        scratch_shapes=[pltpu.VMEM((tm, d), BF16), pltpu.VMEM((tm, d), F32)],
        compiler_params=_params(("parallel", "arbitrary"), _vmem_limit(blocks, scratch, temps)),
        name="ffn",
    )(h, g, w1, w3, w2)


def _ple_glu_kernel(h_ref, p_ref, g_ref, wg_ref, wp_ref, g2_ref, w_ref, b_ref, h_out_ref, u_ref, *, c):
    h = h_ref[...]
    n = _rms(h, g_ref[...]).astype(BF16)
    gate = _sigmoid(jnp.dot(n, wg_ref[...], preferred_element_type=F32))
    emb = jnp.dot(p_ref[...].astype(BF16), wp_ref[...], preferred_element_type=F32)
    h = h + emb * gate
    h_out_ref[...] = h
    n2 = _rms(h, g2_ref[...]).astype(BF16)
    val = jnp.dot(n2, w_ref[:, 0:c], preferred_element_type=F32) + b_ref[:, 0:c]
    glu_gate = jnp.dot(n2, w_ref[:, c:], preferred_element_type=F32) + b_ref[:, c:]
    u_ref[...] = val * _sigmoid(glu_gate)


def _ple_glu(h, p3, layer, g, w_gate, w_ple, g2, w, bias):
    t, d = h.shape
    pd = p3.shape[2]
    c = w.shape[1] // 2
    tm = _tile(t, ROW_TILE)
    row = lambda i: (i, 0)
    fix = lambda i: (0, 0)
    blocks = (2 * _nbytes((tm, d), F32) + _nbytes((tm, pd), F32) + _nbytes((tm, c), F32)
              + _nbytes((d + pd, d), BF16) + _nbytes(w.shape, BF16))
    return pl.pallas_call(
        functools.partial(_ple_glu_kernel, c=c),
        grid=(t // tm,),
        in_specs=[pl.BlockSpec((tm, d), row), pl.BlockSpec((None, tm, pd), lambda i: (layer, i, 0)),
                  pl.BlockSpec((1, d), fix), pl.BlockSpec((d, d), fix), pl.BlockSpec((pd, d), fix),
                  pl.BlockSpec((1, d), fix), pl.BlockSpec(w.shape, fix), pl.BlockSpec((1, 2 * c), fix)],
        out_specs=[pl.BlockSpec((tm, d), row), pl.BlockSpec((tm, c), row)],
        out_shape=[jax.ShapeDtypeStruct((t, d), F32), jax.ShapeDtypeStruct((t, c), F32)],
        compiler_params=_params(("parallel",), _vmem_limit(blocks, 0, 6 * _nbytes((tm, d), F32))),
        name="ple_glu",
    )(h, p3, g, w_gate, w_ple, g2, w, bias)


def _dwconv_kernel(u_ref, h_ref, dw_ref, db_ref, lg_ref, lb_ref, w_ref, b_ref, rg_ref, wr_ref, tri_ref,
                   o_ref, meta_ref, cnt_ref, ext_ref, sh_ref, cv_ref, run_ref, *, ts, kconv, rc, ne):
    halo = CONV_HALO
    sub = V7X_SUBLANES

    @pl.when((pl.program_id(0) == 0) & (pl.program_id(1) == 0))
    def _():
        run_ref[...] = jnp.zeros_like(run_ref)

    @pl.when(pl.program_id(1) == 0)
    def _():
        ext_ref[0:halo, :] = jnp.zeros((halo, ext_ref.shape[1]), F32)

    ext_ref[halo:halo + ts, :] = u_ref[...]
    base = halo - (kconv - 1)
    for r in range(1, sub):
        sh_ref[r - 1] = ext_ref[pl.ds(r, sh_ref.shape[1]), :]

    def tap(k, r0):
        a, r = divmod(base + k, sub)
        start = pl.multiple_of(r0 + a * sub, sub)
        if r == 0:
            return ext_ref[pl.ds(start, rc), :]
        return sh_ref[r - 1, pl.ds(start, rc), :]

    def chunk(i, c):
        r0 = pl.multiple_of(i * rc, rc)
        acc = db_ref[...] + dw_ref[0:1, :] * tap(0, r0)
        for k in range(1, kconv):
            acc = acc + dw_ref[k:k + 1, :] * tap(k, r0)
        cv_ref[pl.ds(r0, rc), :] = acc
        return c

    lax.fori_loop(0, ts // rc, chunk, 0)
    ext_ref[0:halo, :] = ext_ref[ts:ts + halo, :]

    cv = cv_ref[...]
    mu = jnp.mean(cv, axis=-1, keepdims=True)
    xc = cv - mu
    var = jnp.mean(xc * xc, axis=-1, keepdims=True)
    ln = xc * lax.rsqrt(var + EPS) * lg_ref[...] + lb_ref[...]
    act = _silu(ln).astype(BF16)
    h_new = h_ref[...] + jnp.dot(act, w_ref[...], preferred_element_type=F32) + b_ref[...]
    o_ref[...] = h_new
    _route_rows(h_new, rg_ref, wr_ref, tri_ref, run_ref, meta_ref, cnt_ref, ne)


def _dwconv_route(u, h, dw_w, dw_b, ln_g, ln_b, w2, b2, route_g, wr_pad, ne, b, s):
    t, c = u.shape
    d = h.shape[1]
    kconv = dw_w.shape[0]
    assert kconv - 1 <= CONV_HALO
    ts = _tile(s, CONV_ROW_TILE)
    rc = _tile(ts, CONV_ROW_CHUNK)
    nt = s // ts
    kp = -(-kconv // V7X_SUBLANES) * V7X_SUBLANES
    dwp = jnp.zeros((kp, c), F32).at[:kconv].set(dw_w)
    tri = (jnp.arange(ts)[None, :] < jnp.arange(ts)[:, None]).astype(BF16)
    row = lambda i, j: (i * nt + j, 0)
    fix = lambda i, j: (0, 0)
    blocks = (_nbytes((ts, c), F32) + 3 * _nbytes((ts, d), F32) + _nbytes((c, d), BF16) + _nbytes((kp, c), F32)
              + _nbytes((ts, V7X_LANES), F32) + _nbytes(wr_pad.shape, BF16) + _nbytes((ts, ts), BF16))
    sh_rows = ts + CONV_HALO - V7X_SUBLANES
    scratch = _nbytes((ts + CONV_HALO, c), F32) + _nbytes((ts, c), F32) + _nbytes((V7X_SUBLANES - 1, sh_rows, c), F32)
    return pl.pallas_call(
        functools.partial(_dwconv_kernel, ts=ts, kconv=kconv, rc=rc, ne=ne),
        grid=(b, nt),
        in_specs=[pl.BlockSpec((ts, c), row), pl.BlockSpec((ts, d), row), pl.BlockSpec((kp, c), fix),
                  pl.BlockSpec((1, c), fix), pl.BlockSpec((1, c), fix), pl.BlockSpec((1, c), fix),
                  pl.BlockSpec((c, d), fix), pl.BlockSpec((1, d), fix),
                  pl.BlockSpec((1, d), fix), pl.BlockSpec(wr_pad.shape, fix), pl.BlockSpec((ts, ts), fix)],
        out_specs=[pl.BlockSpec((ts, d), row), pl.BlockSpec((ts, V7X_LANES), row),
                   pl.BlockSpec((V7X_SUBLANES, V7X_LANES), fix)],
        out_shape=[jax.ShapeDtypeStruct((t, d), F32), jax.ShapeDtypeStruct((t, V7X_LANES), F32),
                   jax.ShapeDtypeStruct((V7X_SUBLANES, V7X_LANES), F32)],
        scratch_shapes=[pltpu.VMEM((ts + CONV_HALO, c), F32), pltpu.VMEM((V7X_SUBLANES - 1, sh_rows, c), F32),
                        pltpu.VMEM((ts, c), F32), pltpu.VMEM((1, V7X_LANES), F32)],
        compiler_params=_params(("arbitrary", "arbitrary"), _vmem_limit(blocks, scratch, 8 * _nbytes((ts, c), F32))),
        name="dwconv_route",
    )(u, h, dwp, dw_b.reshape(1, c), ln_g.reshape(1, c), ln_b.reshape(1, c), w2, b2.reshape(1, d),
      route_g, wr_pad, tri)


def _route_rows(h, g_ref, wr_ref, tri_ref, run_ref, meta_ref, cnt_ref, ne):
    nf = _rms(h, g_ref[...])
    logits = jnp.dot(nf.astype(BF16), wr_ref[...], preferred_element_type=F32)
    lane = lax.broadcasted_iota(jnp.int32, logits.shape, 1)
    neg = jnp.float32(-jnp.inf)
    logits = jnp.where(lane < ne, logits, neg)
    big = jnp.int32(V7X_LANES)
    m1 = jnp.max(logits, axis=-1, keepdims=True)
    i1 = jnp.min(jnp.where(logits == m1, lane, big), axis=-1, keepdims=True)
    sel1 = lane == i1
    rest = jnp.where(sel1, neg, logits)
    m2 = jnp.max(rest, axis=-1, keepdims=True)
    i2 = jnp.min(jnp.where(rest == m2, lane, big), axis=-1, keepdims=True)
    sel2 = lane == i2
    e2 = jnp.exp(m2 - m1)
    g1 = 1.0 / (1.0 + e2)
    g2 = e2 / (1.0 + e2)
    onehot = jnp.where(sel1 | sel2, 1.0, 0.0)
    before = jnp.dot(tri_ref[...], onehot.astype(BF16), preferred_element_type=F32) + run_ref[...]
    r1 = jnp.sum(jnp.where(sel1, before, 0.0), axis=-1, keepdims=True)
    r2 = jnp.sum(jnp.where(sel2, before, 0.0), axis=-1, keepdims=True)
    rows = onehot.shape[0]
    total = before[rows - 1:rows, :] + onehot[rows - 1:rows, :]
    run_ref[...] = total
    cnt_ref[...] = jnp.broadcast_to(total, cnt_ref.shape)
    meta = jnp.where(lane == 0, g1, jnp.where(lane == 1, g2, 0.0))
    meta = jnp.where(lane == 2, i1.astype(F32), jnp.where(lane == 3, i2.astype(F32), meta))
    meta = jnp.where(lane == 4, r1, jnp.where(lane == 5, r2, meta))
    meta_ref[...] = meta


def _moe_plan(meta, counts_f, tm, ne):
    t = meta.shape[0]
    nt = TOP_K * t // tm + ne
    e_sel = meta[:, 2:2 + TOP_K].astype(jnp.int32)
    rank = meta[:, 2 + TOP_K:2 + 2 * TOP_K].astype(jnp.int32)
    counts = counts_f[0, :ne].astype(jnp.int32)
    padded = ((counts + tm - 1) // tm) * tm
    gend = jnp.cumsum(padded)
    gstart = gend - padded
    slot = gstart[e_sel] + rank
    tile_start = jnp.arange(nt, dtype=jnp.int32) * tm
    tile_e = jnp.sum((tile_start[:, None] >= gend[None, :]).astype(jnp.int32), axis=1)
    valid = (tile_e < ne).astype(jnp.int32)
    return slot, jnp.minimum(tile_e, ne - 1), valid, nt * tm


def _sc_geometry(t, d, dtype):
    sc = pltpu.get_tpu_info().sparse_core
    nc, nw = sc.num_cores, sc.num_cores * sc.num_subcores
    per_w = t // nw
    chunk = _tile(per_w, SC_GATHER_ROWS)
    assert t % nw == 0 and per_w % chunk == 0 and chunk % V7X_SUBLANES == 0
    assert TOP_K * chunk * (d * jnp.dtype(dtype).itemsize + 4) <= sc.vmem_capacity_bytes
    mesh = plsc.VectorSubcoreMesh(core_axis_name="c", subcore_axis_name="s")
    return nc, per_w, chunk, mesh


def _dispatch_rows(rows, slot0, slot1, n_slots):
    t, d = rows.shape
    nc, per_w, chunk, mesh = _sc_geometry(t, d, rows.dtype)

    @functools.partial(
        pl.kernel, mesh=mesh, out_type=jax.ShapeDtypeStruct((n_slots, d), rows.dtype),
        scratch_types=[pltpu.VMEM((chunk,), jnp.int32), pltpu.VMEM((chunk,), jnp.int32),
                       pltpu.VMEM((chunk, d), rows.dtype)])
    def dispatch(rows_hbm, s0_hbm, s1_hbm, out_hbm, i0_v, i1_v, rows_v):
        base = (lax.axis_index("s") * nc + lax.axis_index("c")) * per_w

        @pl.loop(0, per_w // chunk)
        def _(c):
            off = pl.multiple_of(base + c * chunk, chunk)
            pltpu.sync_copy(s0_hbm.at[pl.ds(off, chunk)], i0_v)
            pltpu.sync_copy(s1_hbm.at[pl.ds(off, chunk)], i1_v)
            pltpu.sync_copy(rows_hbm.at[pl.ds(off, chunk)], rows_v)
            pltpu.sync_copy(rows_v, out_hbm.at[i0_v])
            pltpu.sync_copy(rows_v, out_hbm.at[i1_v])

    return dispatch(rows, slot0, slot1)


def _combine_rows(table, slot0, slot1):
    t = slot0.shape[0]
    d = table.shape[1]
    nc, per_w, chunk, mesh = _sc_geometry(t, d, table.dtype)
    out = jax.ShapeDtypeStruct((t, d), table.dtype)

    @functools.partial(
        pl.kernel, mesh=mesh, out_type=(out, out),
        scratch_types=[pltpu.VMEM((chunk,), jnp.int32), pltpu.VMEM((chunk,), jnp.int32),
                       pltpu.VMEM((chunk, d), table.dtype), pltpu.VMEM((chunk, d), table.dtype),
                       pltpu.SemaphoreType.DMA, pltpu.SemaphoreType.DMA])
    def combine(table_hbm, s0_hbm, s1_hbm, y0_hbm, y1_hbm, i0_v, i1_v, r0_v, r1_v, sem0, sem1):
        base = (lax.axis_index("s") * nc + lax.axis_index("c")) * per_w

        @pl.loop(0, per_w // chunk)
        def _(c):
            off = pl.multiple_of(base + c * chunk, chunk)
            pltpu.sync_copy(s0_hbm.at[pl.ds(off, chunk)], i0_v)
            pltpu.sync_copy(s1_hbm.at[pl.ds(off, chunk)], i1_v)
            g0 = pltpu.async_copy(table_hbm.at[i0_v], r0_v, sem0)
            g1 = pltpu.async_copy(table_hbm.at[i1_v], r1_v, sem1)
            g0.wait()
            pltpu.sync_copy(r0_v, y0_hbm.at[pl.ds(off, chunk)])
            g1.wait()
            pltpu.sync_copy(r1_v, y1_hbm.at[pl.ds(off, chunk)])

    return combine(table, slot0, slot1)


def _moe_kernel(te_ref, tv_ref, x_ref, g_ref, w1_ref, w3_ref, w2_ref, o_ref, n_ref, acc_ref):
    i = pl.program_id(0)
    f = pl.program_id(1)
    last = pl.num_programs(1) - 1
    valid = tv_ref[i] == 1

    @pl.when(valid & (f == 0))
    def _():
        n_ref[...] = _rms(x_ref[...], g_ref[...]).astype(BF16)
        acc_ref[...] = jnp.zeros_like(acc_ref)

    @pl.when(valid)
    def _():
        n = n_ref[...]
        w1 = w1_ref[0].astype(BF16)
        w3 = w3_ref[0].astype(BF16)
        w2 = w2_ref[0].astype(BF16)
        a = _silu(jnp.dot(n, w1, preferred_element_type=F32)) * jnp.dot(n, w3, preferred_element_type=F32)
        acc_ref[...] += jnp.dot(a.astype(BF16), w2, preferred_element_type=F32)

    @pl.when(valid & (f == last))
    def _():
        o_ref[...] = acc_ref[...]

    @pl.when(jnp.logical_not(valid) & (f == last))
    def _():
        o_ref[...] = jnp.zeros_like(o_ref)


def _moe_experts(xs, g, tile_e, tile_valid, w1, w3, w2, tm):
    ns, d = xs.shape
    ne, _, ff = w1.shape
    tf = _tile(ff, MOE_COL_TILE, V7X_LANES)
    nf = ff // tf
    fidx = lambda f, tv, i: f * tv[i] + (nf - 1) * (1 - tv[i])
    blocks = 2 * _nbytes((tm, d), F32) + 3 * _nbytes((d, tf), w1.dtype)
    scratch = _nbytes((tm, d), BF16) + _nbytes((tm, d), F32)
    temps = 4 * _nbytes((tm, tf), F32) + _nbytes((tm, d), F32) + 3 * _nbytes((d, tf), BF16)
    grid_spec = pltpu.PrefetchScalarGridSpec(
        num_scalar_prefetch=2,
        grid=(ns // tm, nf),
        in_specs=[pl.BlockSpec((tm, d), lambda i, f, te, tv: (i, 0)),
                  pl.BlockSpec((1, d), lambda i, f, te, tv: (0, 0)),
                  pl.BlockSpec((1, d, tf), lambda i, f, te, tv: (te[i], 0, fidx(f, tv, i))),
                  pl.BlockSpec((1, d, tf), lambda i, f, te, tv: (te[i], 0, fidx(f, tv, i))),
                  pl.BlockSpec((1, tf, d), lambda i, f, te, tv: (te[i], fidx(f, tv, i), 0))],
        out_specs=pl.BlockSpec((tm, d), lambda i, f, te, tv: (i, 0)),
        scratch_shapes=[pltpu.VMEM((tm, d), BF16), pltpu.VMEM((tm, d), F32)])
    return pl.pallas_call(
        _moe_kernel,
        grid_spec=grid_spec,
        out_shape=jax.ShapeDtypeStruct((ns, d), F32),
        compiler_params=_params(("arbitrary", "arbitrary"), _vmem_limit(blocks, scratch, temps)),
        name="moe_experts",
    )(tile_e, tile_valid, xs, g, w1, w3, w2)


def _combine_ple_kernel(h_ref, y0_ref, y1_ref, m_ref, p_ref, g_ref, wg_ref, wp_ref, fg_ref, o_ref):
    meta = m_ref[...]
    h = h_ref[...] + meta[:, 0:1] * y0_ref[...] + meta[:, 1:2] * y1_ref[...]
    n = _rms(h, g_ref[...]).astype(BF16)
    gate = _sigmoid(jnp.dot(n, wg_ref[...], preferred_element_type=F32))
    emb = jnp.dot(p_ref[...].astype(BF16), wp_ref[...], preferred_element_type=F32)
    o_ref[...] = _rms(h + emb * gate, fg_ref[...])


def _combine_ple(h, y0, y1, meta, p3, layer, g, w_gate, w_ple, final_g):
    t, d = h.shape
    pd = p3.shape[2]
    tm = _tile(t, ROW_TILE)
    row = lambda i: (i, 0)
    fix = lambda i: (0, 0)
    blocks = (4 * _nbytes((tm, d), F32) + _nbytes((tm, pd + V7X_LANES), F32) + _nbytes((d + pd, d), BF16))
    return pl.pallas_call(
        _combine_ple_kernel,
        grid=(t // tm,),
        in_specs=[pl.BlockSpec((tm, d), row), pl.BlockSpec((tm, d), row), pl.BlockSpec((tm, d), row),
                  pl.BlockSpec((tm, V7X_LANES), row),
                  pl.BlockSpec((None, tm, pd), lambda i: (layer, i, 0)), pl.BlockSpec((1, d), fix),
                  pl.BlockSpec((d, d), fix), pl.BlockSpec((pd, d), fix), pl.BlockSpec((1, d), fix)],
        out_specs=pl.BlockSpec((tm, d), row),
        out_shape=jax.ShapeDtypeStruct((t, d), F32),
        compiler_params=_params(("parallel",), _vmem_limit(blocks, 0, 4 * _nbytes((tm, d), F32))),
        name="combine_ple_final",
    )(h, y0, y1, meta, p3, g, w_gate, w_ple, final_g)


def kernel(x, p, mix_norm_g, ffn_norm_g, ple_norm_g, w_ple, w_ple_gate, ev_w_in, ev_conv_w, ev_conv_b, ev_dt_bias, ev_a_log, ev_d, ev_ssd_norm_g, ev_w_out, ev_ffn_w1, ev_ffn_w3, ev_ffn_w2, od_pw1_w, od_pw1_b, od_dw_w, od_dw_b, od_ln_g, od_ln_b, od_pw2_w, od_pw2_b, od_router_w, od_moe_w1, od_moe_w3, od_moe_w2, final_norm_g):
    b, s, d = x.shape
    t = b * s
    depth = p.shape[0]
    assert depth == 2 and ev_w_in.shape[0] == 1 and od_pw1_w.shape[0] == 1
    sbw = SB_HEADS * SB_HEAD_DIM
    ssdw = SSD_HEADS * SSD_HEAD_DIM
    xbcw = ssdw + 2 * SSD_GROUPS * SSD_STATE
    in_width = 3 * sbw + ssdw + xbcw + SSD_HEADS
    assert ev_w_in.shape[2] == in_width
    bf = lambda w: w.astype(BF16)
    vec = lambda v: v.reshape(1, -1)

    x2 = x.reshape(t, d)
    p3 = p.reshape(depth, t, -1)

    w_all = bf(ev_w_in[0])
    w_rest = w_all[:, 3 * sbw:]
    w_in = jnp.concatenate([w_all[:, :2 * sbw], w_rest,
                            jnp.zeros((d, V7X_LANES - SSD_HEADS), BF16)], axis=1)
    w_vt = w_all[:, 2 * sbw:3 * sbw].T
    qscale = LOG2E / math.sqrt(SB_HEAD_DIM)
    qk, vt, z, xbc, dt = _in_proj(x2, vec(mix_norm_g[0]), w_in, w_vt, sbw=sbw, zw=ssdw, xbcw=xbcw, qscale=qscale)
    attn = _sb_attention(qk, vt, b, s, heads=SB_HEADS, hd=SB_HEAD_DIM)
    ynorm = _ssd(xbc, z, dt, ev_conv_w[0], ev_conv_b[0], ev_dt_bias[0], ev_a_log[0], ev_d[0], ev_ssd_norm_g[0],
                 b, s, heads=SSD_HEADS, hd=SSD_HEAD_DIM, nstate=SSD_STATE)
    h = _out_proj(x2, attn, ynorm, bf(ev_w_out[0]))
    h = _ffn(h, vec(ffn_norm_g[0]), bf(ev_ffn_w1[0]), bf(ev_ffn_w3[0]), bf(ev_ffn_w2[0]))
    h, u = _ple_glu(h, p3, 0, vec(ple_norm_g[0]), bf(w_ple_gate[0]), bf(w_ple[0]),
                    vec(mix_norm_g[1]), bf(od_pw1_w[0]), vec(od_pw1_b[0]))
    ne = od_router_w.shape[2]
    wr = jnp.zeros((d, V7X_LANES), BF16).at[:, :ne].set(bf(od_router_w[0]))
    h, meta, counts = _dwconv_route(u, h, od_dw_w[0], od_dw_b[0], od_ln_g[0], od_ln_b[0], bf(od_pw2_w[0]),
                                       od_pw2_b[0], vec(ffn_norm_g[1]), wr, ne, b, s)
    tm = _tile(TOP_K * t, FFN_ROW_TILE)
    slot, tile_e, tile_valid, n_slots = _moe_plan(meta, counts, tm, ne)
    xs = _dispatch_rows(h, slot[:, 0], slot[:, 1], n_slots)
    ys = _moe_experts(xs, vec(ffn_norm_g[1]), tile_e, tile_valid, od_moe_w1[0], od_moe_w3[0], od_moe_w2[0], tm)
    y0, y1 = _combine_rows(ys, slot[:, 0], slot[:, 1])
    out = _combine_ple(h, y0, y1, meta, p3, 1, vec(ple_norm_g[1]), bf(w_ple_gate[1]), bf(w_ple[1]), vec(final_norm_g))
    return out.reshape(b, s, d)
```

```python
import functools
import math

import jax
import jax.numpy as jnp
from jax import lax
from jax.experimental import pallas as pl
from jax.experimental.pallas import tpu as pltpu
from jax.experimental.pallas import tpu_sc as plsc

F32 = jnp.float32
BF16 = jnp.bfloat16
EPS = 1e-6
LOG2E = 1.4426950408889634

V7X_LANES = 128
V7X_SUBLANES = 8
V7X_VMEM_BYTES = 64 * 1024 * 1024

SB_HEADS = 8
SB_HEAD_DIM = 64
SSD_HEADS = 8
SSD_HEAD_DIM = 64
SSD_STATE = 128
SSD_GROUPS = 2
SSD_CONV = 4
N_EXPERTS = 8
TOP_K = 2

ROW_TILE = 1024
FFN_ROW_TILE = 1024
FFN_COL_TILE = 1408
MOE_COL_TILE = 512
ATTN_BLOCK = 256
ATTN_PAIRS = 2
SSD_CHUNK = 128
CONV_ROW_TILE = 512
CONV_ROW_CHUNK = 32
CONV_HALO = 32
SC_GATHER_ROWS = 32


def _tile(n, pref, mult=V7X_SUBLANES):
    if n <= pref:
        return n
    t = (pref // mult) * mult
    while t > mult and n % t:
        t -= mult
    assert n % t == 0, (n, pref, mult)
    return t


def _vmem_limit(block_bytes, scratch_bytes=0, temp_bytes=0):
    need = 2 * block_bytes + scratch_bytes + temp_bytes + (4 << 20)
    return int(min(max(need, 16 << 20), V7X_VMEM_BYTES - (6 << 20)))


def _params(sem, vmem):
    return pltpu.CompilerParams(dimension_semantics=sem, vmem_limit_bytes=vmem)


def _nbytes(shape, dtype):
    return math.prod(shape) * jnp.dtype(dtype).itemsize


def _rms(x, g):
    return x * lax.rsqrt(jnp.mean(x * x, axis=-1, keepdims=True) + EPS) * g


def _sigmoid(x):
    return 1.0 / (1.0 + jnp.exp(-x))


def _silu(x):
    return x * _sigmoid(x)


def _softplus(x):
    return jnp.maximum(x, 0.0) + jnp.log(1.0 + jnp.exp(-jnp.abs(x)))


def _split3(x):
    hi = x.astype(BF16)
    r1 = x - hi.astype(F32)
    mid = r1.astype(BF16)
    lo = (r1 - mid.astype(F32)).astype(BF16)
    return hi, mid, lo


def _dot_exact_rhs(x, m):
    hi, mid, lo = _split3(x)
    d = functools.partial(jnp.dot, preferred_element_type=F32)
    return d(hi, m) + d(mid, m) + d(lo, m)


def _dot_exact_lhs(m, x):
    hi, mid, lo = _split3(x)
    d = functools.partial(jnp.dot, preferred_element_type=F32)
    return d(m, hi) + d(m, mid) + d(m, lo)


def _in_proj_kernel(x_ref, g_ref, w_ref, wvt_ref, qk_ref, vt_ref, z_ref, xbc_ref, dt_ref, *, sbw, zw, xbcw, qscale):
    hn = _rms(x_ref[...], g_ref[...]).astype(BF16)

    def mm(lo, hi):
        return jnp.dot(hn, w_ref[:, lo:hi], preferred_element_type=F32)

    qk_ref[:, 0:sbw] = (mm(0, sbw) * qscale).astype(BF16)
    qk_ref[:, sbw:2 * sbw] = mm(sbw, 2 * sbw).astype(BF16)
    vt_ref[...] = lax.dot_general(wvt_ref[...], hn, (((1,), (1,)), ((), ())),
                                  preferred_element_type=F32).astype(BF16)
    o = 2 * sbw
    z_ref[...] = mm(o, o + zw)
    o += zw
    xbc_ref[...] = mm(o, o + xbcw)
    o += xbcw
    dt_ref[...] = mm(o, o + V7X_LANES)


def _in_proj(x2, g, w_pad, w_vt, *, sbw, zw, xbcw, qscale):
    t, d = x2.shape
    n = w_pad.shape[1]
    tm = _tile(t, ROW_TILE, V7X_LANES)
    blocks = (_nbytes((tm, d), F32) + _nbytes((d, n + sbw), BF16) + _nbytes((tm, 3 * sbw), BF16)
              + _nbytes((tm, zw + xbcw + V7X_LANES), F32))
    return pl.pallas_call(
        functools.partial(_in_proj_kernel, sbw=sbw, zw=zw, xbcw=xbcw, qscale=qscale),
        grid=(t // tm,),
        in_specs=[pl.BlockSpec((tm, d), lambda i: (i, 0)),
                  pl.BlockSpec((1, d), lambda i: (0, 0)),
                  pl.BlockSpec((d, n), lambda i: (0, 0)),
                  pl.BlockSpec((sbw, d), lambda i: (0, 0))],
        out_specs=[pl.BlockSpec((tm, 2 * sbw), lambda i: (i, 0)),
                   pl.BlockSpec((sbw, tm), lambda i: (0, i)),
                   pl.BlockSpec((tm, zw), lambda i: (i, 0)),
                   pl.BlockSpec((tm, xbcw), lambda i: (i, 0)),
                   pl.BlockSpec((tm, V7X_LANES), lambda i: (i, 0))],
        out_shape=[jax.ShapeDtypeStruct((t, 2 * sbw), BF16),
                   jax.ShapeDtypeStruct((sbw, t), BF16),
                   jax.ShapeDtypeStruct((t, zw), F32),
                   jax.ShapeDtypeStruct((t, xbcw), F32),
                   jax.ShapeDtypeStruct((t, V7X_LANES), F32)],
        compiler_params=_params(("parallel",), _vmem_limit(blocks, 0, _nbytes((tm, n), F32))),
        name="in_proj",
    )(x2, g, w_pad, w_vt)


ATTN_MASKED = -1e30


def _sb_attn_kernel(q_ref, k_ref, vt_ref, m_ref, o_ref, *scratch, blk, hd, npair):
    qi = pl.program_id(2)
    nblocks = qi + 1
    width = 2 * hd
    nh = 2 * npair
    acc_refs, car_refs, z_refs, incl_refs = (scratch[i * nh:(i + 1) * nh] for i in range(4))
    lane = lax.broadcasted_iota(jnp.int32, (1, width), 1)
    qh = []
    for p in range(npair):
        q = q_ref[:, p * width:(p + 1) * width]
        qz = jnp.zeros_like(q)
        qh += [jnp.where(lane < hd, q, qz), jnp.where(lane >= hd, q, qz)]
    sign = jnp.uint32(0x80000000)

    def key_start(m):
        j = jnp.where(m < nblocks, qi - m, qi)
        return pl.multiple_of(j * blk, blk)

    def scores(m, h):
        p = h // 2
        kblk = k_ref[pl.ds(key_start(m), blk), p * width:(p + 1) * width]
        return lax.dot_general(kblk, qh[h], (((1,), (1,)), ((), ())), preferred_element_type=F32)

    def cumsum_stage(h, z_cur, i_cur):
        nabs = pltpu.bitcast(pltpu.bitcast(z_cur, jnp.uint32) | sign, F32)
        sp2 = jnp.maximum(z_cur, 0.0) + jnp.log2(1.0 + jnp.exp2(nabs))
        incl_refs[h][i_cur] = jnp.dot(m_ref[...], sp2.astype(BF16), preferred_element_type=F32)

    def weight_stage(h, z_prev, incl, vstart):
        p = h // 2
        car = car_refs[h][...]
        w = jnp.exp2(z_prev - incl - car)
        vtblk = vt_ref[p * width:(p + 1) * width, pl.ds(vstart, blk)]
        acc_refs[h][...] += jnp.dot(vtblk, w.astype(BF16), preferred_element_type=F32)
        car_refs[h][...] = car + incl[0:1, :]

    row = lax.broadcasted_iota(jnp.int32, (blk, blk), 0)
    col = lax.broadcasted_iota(jnp.int32, (blk, blk), 1)
    for h in range(nh):
        acc_refs[h][...] = jnp.zeros_like(acc_refs[h])
        car_refs[h][...] = jnp.zeros_like(car_refs[h])
        z_refs[h][0] = jnp.where(row < col, scores(0, h), ATTN_MASKED)

    def first(m, c):
        for h in range(nh):
            cumsum_stage(h, z_refs[h][0], 0)
            z_refs[h][1] = scores(1, h)
        return c

    lax.fori_loop(0, jnp.minimum(nblocks, 1), first, 0)

    def body(m, c):
        s_next = lax.rem(m + 1, 3)
        s_cur = lax.rem(m, 3)
        s_prev = lax.rem(m + 2, 3)
        i_cur = lax.rem(m, 2)
        i_prev = lax.rem(m + 1, 2)
        vstart = key_start(m - 1)
        for h in range(nh):
            z_prev = z_refs[h][s_prev]
            incl = incl_refs[h][i_prev]
            z_cur = z_refs[h][s_cur]
            weight_stage(h, z_prev, incl, vstart)
            cumsum_stage(h, z_cur, i_cur)
            z_refs[h][s_next] = scores(m + 1, h)
        return c

    lax.fori_loop(1, nblocks, body, 0)
    last = nblocks - 1
    for h in range(nh):
        weight_stage(h, z_refs[h][lax.rem(last, 3)], incl_refs[h][lax.rem(last, 2)], key_start(last))
    srow = lax.broadcasted_iota(jnp.int32, (width, 1), 0)
    for p in range(npair):
        both = jnp.where(srow < hd, acc_refs[2 * p][...], acc_refs[2 * p + 1][...])
        o_ref[:, p * width:(p + 1) * width] = both.T.astype(o_ref.dtype)


def _sb_attention(qk, vt, b, s, *, heads, hd):
    t = qk.shape[0]
    blk = _tile(s, ATTN_BLOCK, V7X_LANES)
    nq = s // blk
    width = 2 * hd
    assert width == V7X_LANES and heads % (2 * ATTN_PAIRS) == 0
    npair = ATTN_PAIRS
    gw = npair * width
    ng = heads * hd // gw
    m_t = (jnp.arange(blk)[None, :] >= jnp.arange(blk)[:, None]).astype(BF16)
    blocks = 2 * _nbytes((blk, gw), BF16) + 2 * _nbytes((s, gw), BF16) + _nbytes((blk, blk), BF16)
    nh = 2 * npair
    scratch = nh * (_nbytes((width, blk), F32) + _nbytes((V7X_SUBLANES, blk), F32) + 5 * _nbytes((blk, blk), F32))
    temps = 4 * nh * _nbytes((blk, blk), F32)
    return pl.pallas_call(
        functools.partial(_sb_attn_kernel, blk=blk, hd=hd, npair=npair),
        grid=(b, ng, nq),
        in_specs=[pl.BlockSpec((blk, gw), lambda bi, g, qi: (bi * nq + qi, g)),
                  pl.BlockSpec((s, gw), lambda bi, g, qi: (bi, ng + g)),
                  pl.BlockSpec((gw, s), lambda bi, g, qi: (g, bi)),
                  pl.BlockSpec((blk, blk), lambda bi, g, qi: (0, 0))],
        out_specs=pl.BlockSpec((blk, gw), lambda bi, g, qi: (bi * nq + qi, g)),
        out_shape=jax.ShapeDtypeStruct((t, heads * hd), BF16),
        scratch_shapes=([pltpu.VMEM((width, blk), F32)] * nh + [pltpu.VMEM((1, blk), F32)] * nh
                        + [pltpu.VMEM((3, blk, blk), F32)] * nh + [pltpu.VMEM((2, blk, blk), F32)] * nh),
        compiler_params=_params(("parallel", "parallel", "arbitrary"), _vmem_limit(blocks, scratch, temps)),
        name="sb_attention",
    )(qk, qk, vt, m_t)


def _ssd_kernel(xbc_ref, z_ref, dt_ref, cw_ref, cb_ref, dtb_ref, alog_ref, dskip_ref, ng_ref, tri_ref, exp_ref,
                o_ref, ext_ref, st_ref, *, q, width, nstate, hd, kconv):
    halo = V7X_SUBLANES
    gw = width // SSD_GROUPS

    @pl.when(pl.program_id(1) == 0)
    def _():
        ext_ref[0:halo, :] = jnp.zeros((halo, ext_ref.shape[1]), F32)
        st_ref[...] = jnp.zeros_like(st_ref)

    x_new = xbc_ref[...]
    ext_ref[halo:halo + q, :] = x_new
    conv = cb_ref[...] + cw_ref[0:1, :] * ext_ref[pl.ds(halo - (kconv - 1), q), :]
    for k in range(1, kconv):
        conv = conv + cw_ref[k:k + 1, :] * ext_ref[pl.ds(halo - (kconv - 1) + k, q), :]
    ext_ref[0:halo, :] = x_new[q - halo:q, :]
    xbc = _silu(conv)
    xs = xbc[:, :width]
    bm = xbc[:, width:width + SSD_GROUPS * nstate]
    cm = xbc[:, width + SSD_GROUPS * nstate:]

    dt = _softplus(dt_ref[...] + dtb_ref[...])
    a = -jnp.exp(alog_ref[...])
    acum = _dot_exact_lhs(tri_ref[...], dt * a)
    ea = jnp.exp(acum)
    dte = jnp.exp(acum[q - 1:q, :] - acum)
    ex = exp_ref[...]
    dt_x = _dot_exact_rhs(dt, ex)
    ea_x = _dot_exact_rhs(ea, ex)
    dte_x = _dot_exact_rhs(dte, ex)
    xdt = xs * dt_x
    xdt_b = xdt.astype(BF16)
    xdec_b = (xdt * dte_x).astype(BF16)
    acum_t = acum.T

    row = lax.broadcasted_iota(jnp.int32, (q, q), 0)
    col = lax.broadcasted_iota(jnp.int32, (q, q), 1)
    causal = col <= row
    lane = lax.broadcasted_iota(jnp.int32, (1, 2 * hd), 1)
    heads_per_group = gw // hd
    y_parts = []
    for g in range(SSD_GROUPS):
        bg = bm[:, g * nstate:(g + 1) * nstate].astype(BF16)
        cg = cm[:, g * nstate:(g + 1) * nstate].astype(BF16)
        gmat = lax.dot_general(cg, bg, (((1,), (1,)), ((), ())), preferred_element_type=F32)
        for pr in range(heads_per_group // 2):
            c0 = g * gw + pr * 2 * hd
            xpair = xdt_b[:, c0:c0 + 2 * hd]
            yd = []
            for hh in range(2):
                h = (c0 // hd) + hh
                seg = acum[:, h:h + 1] - acum_t[h:h + 1, :]
                lmat = jnp.exp(jnp.where(causal, seg, -jnp.inf))
                yd.append(jnp.dot((gmat * lmat).astype(BF16), xpair, preferred_element_type=F32))
            y_parts.append(jnp.where(lane < hd, yd[0], yd[1]))
        hprev = st_ref[:, g * gw:(g + 1) * gw]
        y_off = jnp.dot(cg, hprev.astype(BF16), preferred_element_type=F32) * ea_x[:, g * gw:(g + 1) * gw]
        st_new = lax.dot_general(bg, xdec_b[:, g * gw:(g + 1) * gw], (((0,), (0,)), ((), ())),
                                 preferred_element_type=F32)
        st_ref[:, g * gw:(g + 1) * gw] = hprev * ea_x[q - 1:q, g * gw:(g + 1) * gw] + st_new
        y_parts.append(y_off)
    pp = heads_per_group // 2 + 1
    ys = []
    for g in range(SSD_GROUPS):
        diag = jnp.concatenate(y_parts[g * pp:g * pp + pp - 1], axis=1)
        ys.append(diag + y_parts[g * pp + pp - 1])
    y = jnp.concatenate(ys, axis=1) + xs * dskip_ref[...]
    o_ref[...] = _rms(y * _silu(z_ref[...]), ng_ref[...]).astype(o_ref.dtype)


def _ssd(xbc, z, dt, conv_w, conv_b, dt_bias, a_log, d_skip, norm_g, b, s, *, heads, hd, nstate):
    t, xbcw = xbc.shape
    width = heads * hd
    q = _tile(s, SSD_CHUNK, V7X_LANES)
    nc = s // q
    kconv = conv_w.shape[0]
    cw = jnp.zeros((V7X_SUBLANES, xbcw), F32).at[:kconv].set(conv_w)
    pad = lambda v: jnp.zeros((1, V7X_LANES), F32).at[0, :heads].set(v)
    tri = (jnp.arange(q)[None, :] <= jnp.arange(q)[:, None]).astype(BF16)
    hx = (jnp.arange(V7X_LANES)[:, None] == (jnp.arange(width)[None, :] // hd)).astype(BF16)
    row = lambda i, c: (i * nc + c, 0)
    fix = lambda i, c: (0, 0)
    blocks = (_nbytes((q, xbcw + width + V7X_LANES), F32) + _nbytes((q, width), BF16)
              + _nbytes((q, q), BF16) + _nbytes((V7X_LANES, width), BF16) + 8 * _nbytes((1, xbcw), F32))
    scratch = _nbytes((q + V7X_SUBLANES, xbcw), F32) + _nbytes((nstate, width), F32)
    temps = 24 * _nbytes((q, xbcw), F32)
    return pl.pallas_call(
        functools.partial(_ssd_kernel, q=q, width=width, nstate=nstate, hd=hd, kconv=kconv),
        grid=(b, nc),
        in_specs=[pl.BlockSpec((q, xbcw), row), pl.BlockSpec((q, width), row), pl.BlockSpec((q, V7X_LANES), row),
                  pl.BlockSpec((V7X_SUBLANES, xbcw), fix), pl.BlockSpec((1, xbcw), fix),
                  pl.BlockSpec((1, V7X_LANES), fix), pl.BlockSpec((1, V7X_LANES), fix),
                  pl.BlockSpec((1, width), fix), pl.BlockSpec((1, width), fix),
                  pl.BlockSpec((q, q), fix), pl.BlockSpec((V7X_LANES, width), fix)],
        out_specs=pl.BlockSpec((q, width), row),
        out_shape=jax.ShapeDtypeStruct((t, width), BF16),
        scratch_shapes=[pltpu.VMEM((q + V7X_SUBLANES, xbcw), F32), pltpu.VMEM((nstate, width), F32)],
        compiler_params=_params(("parallel", "arbitrary"), _vmem_limit(blocks, scratch, temps)),
        name="ssd",
    )(xbc, z, dt, cw, conv_b.reshape(1, xbcw), pad(dt_bias), pad(a_log),
      jnp.repeat(d_skip, hd).reshape(1, width), norm_g.reshape(1, width), tri, hx)


def _out_proj_kernel(x_ref, a_ref, y_ref, w_ref, o_ref, *, wa):
    acc = jnp.dot(a_ref[...], w_ref[0:wa, :], preferred_element_type=F32)
    acc = acc + jnp.dot(y_ref[...], w_ref[wa:, :], preferred_element_type=F32)
    o_ref[...] = x_ref[...] + acc


def _out_proj(x2, attn, ynorm, w):
    t, d = x2.shape
    wa, wy = attn.shape[1], ynorm.shape[1]
    tm = _tile(t, ROW_TILE)
    row = lambda i: (i, 0)
    blocks = 2 * _nbytes((tm, d), F32) + _nbytes((tm, wa + wy), BF16) + _nbytes(w.shape, BF16)
    return pl.pallas_call(
        functools.partial(_out_proj_kernel, wa=wa),
        grid=(t // tm,),
        in_specs=[pl.BlockSpec((tm, d), row), pl.BlockSpec((tm, wa), row), pl.BlockSpec((tm, wy), row),
                  pl.BlockSpec(w.shape, lambda i: (0, 0))],
        out_specs=pl.BlockSpec((tm, d), row),
        out_shape=jax.ShapeDtypeStruct((t, d), F32),
        compiler_params=_params(("parallel",), _vmem_limit(blocks, 0, 2 * _nbytes((tm, d), F32))),
        name="out_proj",
    )(x2, attn, ynorm, w)


def _ffn_kernel(h_ref, g_ref, w1_ref, w3_ref, w2_ref, o_ref, n_ref, acc_ref):
    f = pl.program_id(1)

    @pl.when(f == 0)
    def _():
        n_ref[...] = _rms(h_ref[...], g_ref[...]).astype(BF16)
        acc_ref[...] = jnp.zeros_like(acc_ref)

    n = n_ref[...]
    a = _silu(jnp.dot(n, w1_ref[...], preferred_element_type=F32)) * jnp.dot(n, w3_ref[...], preferred_element_type=F32)
    acc_ref[...] += jnp.dot(a.astype(BF16), w2_ref[...], preferred_element_type=F32)

    @pl.when(f == pl.num_programs(1) - 1)
    def _():
        o_ref[...] = h_ref[...] + acc_ref[...]


def _ffn(h, g, w1, w3, w2):
    t, d = h.shape
    ff = w1.shape[1]
    tm = _tile(t, FFN_ROW_TILE)
    tf = _tile(ff, FFN_COL_TILE, V7X_LANES)
    blocks = 2 * _nbytes((tm, d), F32) + 3 * _nbytes((d, tf), BF16)
    scratch = _nbytes((tm, d), BF16) + _nbytes((tm, d), F32)
    temps = 4 * _nbytes((tm, tf), F32) + _nbytes((tm, d), F32)
    return pl.pallas_call(
        _ffn_kernel,
        grid=(t // tm, ff // tf),
        in_specs=[pl.BlockSpec((tm, d), lambda i, f: (i, 0)), pl.BlockSpec((1, d), lambda i, f: (0, 0)),
                  pl.BlockSpec((d, tf), lambda i, f: (0, f)), pl.BlockSpec((d, tf), lambda i, f: (0, f)),
                  pl.BlockSpec((tf, d), lambda i, f: (f, 0))],
        out_specs=pl.BlockSpec((tm, d), lambda i, f: (i, 0)),
        out_shape=jax.ShapeDtypeStruct((t, d), F32),
        scratch_shapes=[pltpu.VMEM((tm, d), BF16), pltpu.VMEM((tm, d), F32)],
        compiler_params=_params(("parallel", "arbitrary"), _vmem_limit(blocks, scratch, temps)),
        name="ffn",
    )(h, g, w1, w3, w2)


def _ple_glu_kernel(h_ref, p_ref, g_ref, wg_ref, wp_ref, g2_ref, w_ref, b_ref, h_out_ref, u_ref, *, c):
    h = h_ref[...]
    n = _rms(h, g_ref[...]).astype(BF16)
    gate = _sigmoid(jnp.dot(n, wg_ref[...], preferred_element_type=F32))
    emb = jnp.dot(p_ref[...].astype(BF16), wp_ref[...], preferred_element_type=F32)
    h = h + emb * gate
    h_out_ref[...] = h
    n2 = _rms(h, g2_ref[...]).astype(BF16)
    val = jnp.dot(n2, w_ref[:, 0:c], preferred_element_type=F32) + b_ref[:, 0:c]
    glu_gate = jnp.dot(n2, w_ref[:, c:], preferred_element_type=F32) + b_ref[:, c:]
    u_ref[...] = val * _sigmoid(glu_gate)


def _ple_glu(h, p3, layer, g, w_gate, w_ple, g2, w, bias):
    t, d = h.shape
    pd = p3.shape[2]
    c = w.shape[1] // 2
    tm = _tile(t, ROW_TILE)
    row = lambda i: (i, 0)
    fix = lambda i: (0, 0)
    blocks = (2 * _nbytes((tm, d), F32) + _nbytes((tm, pd), F32) + _nbytes((tm, c), F32)
              + _nbytes((d + pd, d), BF16) + _nbytes(w.shape, BF16))
    return pl.pallas_call(
        functools.partial(_ple_glu_kernel, c=c),
        grid=(t // tm,),
        in_specs=[pl.BlockSpec((tm, d), row), pl.BlockSpec((None, tm, pd), lambda i: (layer, i, 0)),
                  pl.BlockSpec((1, d), fix), pl.BlockSpec((d, d), fix), pl.BlockSpec((pd, d), fix),
                  pl.BlockSpec((1, d), fix), pl.BlockSpec(w.shape, fix), pl.BlockSpec((1, 2 * c), fix)],
        out_specs=[pl.BlockSpec((tm, d), row), pl.BlockSpec((tm, c), row)],
        out_shape=[jax.ShapeDtypeStruct((t, d), F32), jax.ShapeDtypeStruct((t, c), F32)],
        compiler_params=_params(("parallel",), _vmem_limit(blocks, 0, 6 * _nbytes((tm, d), F32))),
        name="ple_glu",
    )(h, p3, g, w_gate, w_ple, g2, w, bias)


def _dwconv_kernel(u_ref, h_ref, dw_ref, db_ref, lg_ref, lb_ref, w_ref, b_ref, rg_ref, wr_ref, tri_ref,
                   o_ref, meta_ref, cnt_ref, ext_ref, sh_ref, cv_ref, run_ref, *, ts, kconv, rc, ne):
    halo = CONV_HALO
    sub = V7X_SUBLANES

    @pl.when((pl.program_id(0) == 0) & (pl.program_id(1) == 0))
    def _():
        run_ref[...] = jnp.zeros_like(run_ref)

    @pl.when(pl.program_id(1) == 0)
    def _():
        ext_ref[0:halo, :] = jnp.zeros((halo, ext_ref.shape[1]), F32)

    ext_ref[halo:halo + ts, :] = u_ref[...]
    base = halo - (kconv - 1)
    for r in range(1, sub):
        sh_ref[r - 1] = ext_ref[pl.ds(r, sh_ref.shape[1]), :]

    def tap(k, r0):
        a, r = divmod(base + k, sub)
        start = pl.multiple_of(r0 + a * sub, sub)
        if r == 0:
            return ext_ref[pl.ds(start, rc), :]
        return sh_ref[r - 1, pl.ds(start, rc), :]

    def chunk(i, c):
        r0 = pl.multiple_of(i * rc, rc)
        acc = db_ref[...] + dw_ref[0:1, :] * tap(0, r0)
        for k in range(1, kconv):
            acc = acc + dw_ref[k:k + 1, :] * tap(k, r0)
        cv_ref[pl.ds(r0, rc), :] = acc
        return c

    lax.fori_loop(0, ts // rc, chunk, 0)
    ext_ref[0:halo, :] = ext_ref[ts:ts + halo, :]

    cv = cv_ref[...]
    mu = jnp.mean(cv, axis=-1, keepdims=True)
    xc = cv - mu
    var = jnp.mean(xc * xc, axis=-1, keepdims=True)
    ln = xc * lax.rsqrt(var + EPS) * lg_ref[...] + lb_ref[...]
    act = _silu(ln).astype(BF16)
    h_new = h_ref[...] + jnp.dot(act, w_ref[...], preferred_element_type=F32) + b_ref[...]
    o_ref[...] = h_new
    _route_rows(h_new, rg_ref, wr_ref, tri_ref, run_ref, meta_ref, cnt_ref, ne)


def _dwconv_route(u, h, dw_w, dw_b, ln_g, ln_b, w2, b2, route_g, wr_pad, ne, b, s):
    t, c = u.shape
    d = h.shape[1]
    kconv = dw_w.shape[0]
    assert kconv - 1 <= CONV_HALO
    ts = _tile(s, CONV_ROW_TILE)
    rc = _tile(ts, CONV_ROW_CHUNK)
    nt = s // ts
    kp = -(-kconv // V7X_SUBLANES) * V7X_SUBLANES
    dwp = jnp.zeros((kp, c), F32).at[:kconv].set(dw_w)
    tri = (jnp.arange(ts)[None, :] < jnp.arange(ts)[:, None]).astype(BF16)
    row = lambda i, j: (i * nt + j, 0)
    fix = lambda i, j: (0, 0)
    blocks = (_nbytes((ts, c), F32) + 3 * _nbytes((ts, d), F32) + _nbytes((c, d), BF16) + _nbytes((kp, c), F32)
              + _nbytes((ts, V7X_LANES), F32) + _nbytes(wr_pad.shape, BF16) + _nbytes((ts, ts), BF16))
    sh_rows = ts + CONV_HALO - V7X_SUBLANES
    scratch = _nbytes((ts + CONV_HALO, c), F32) + _nbytes((ts, c), F32) + _nbytes((V7X_SUBLANES - 1, sh_rows, c), F32)
    return pl.pallas_call(
        functools.partial(_dwconv_kernel, ts=ts, kconv=kconv, rc=rc, ne=ne),
        grid=(b, nt),
        in_specs=[pl.BlockSpec((ts, c), row), pl.BlockSpec((ts, d), row), pl.BlockSpec((kp, c), fix),
                  pl.BlockSpec((1, c), fix), pl.BlockSpec((1, c), fix), pl.BlockSpec((1, c), fix),
                  pl.BlockSpec((c, d), fix), pl.BlockSpec((1, d), fix),
                  pl.BlockSpec((1, d), fix), pl.BlockSpec(wr_pad.shape, fix), pl.BlockSpec((ts, ts), fix)],
        out_specs=[pl.BlockSpec((ts, d), row), pl.BlockSpec((ts, V7X_LANES), row),
                   pl.BlockSpec((V7X_SUBLANES, V7X_LANES), fix)],
        out_shape=[jax.ShapeDtypeStruct((t, d), F32), jax.ShapeDtypeStruct((t, V7X_LANES), F32),
                   jax.ShapeDtypeStruct((V7X_SUBLANES, V7X_LANES), F32)],
        scratch_shapes=[pltpu.VMEM((ts + CONV_HALO, c), F32), pltpu.VMEM((V7X_SUBLANES - 1, sh_rows, c), F32),
                        pltpu.VMEM((ts, c), F32), pltpu.VMEM((1, V7X_LANES), F32)],
        compiler_params=_params(("arbitrary", "arbitrary"), _vmem_limit(blocks, scratch, 8 * _nbytes((ts, c), F32))),
        name="dwconv_route",
    )(u, h, dwp, dw_b.reshape(1, c), ln_g.reshape(1, c), ln_b.reshape(1, c), w2, b2.reshape(1, d),
      route_g, wr_pad, tri)


def _route_rows(h, g_ref, wr_ref, tri_ref, run_ref, meta_ref, cnt_ref, ne):
    nf = _rms(h, g_ref[...])
    logits = jnp.dot(nf.astype(BF16), wr_ref[...], preferred_element_type=F32)
    lane = lax.broadcasted_iota(jnp.int32, logits.shape, 1)
    neg = jnp.float32(-jnp.inf)
    logits = jnp.where(lane < ne, logits, neg)
    big = jnp.int32(V7X_LANES)
    m1 = jnp.max(logits, axis=-1, keepdims=True)
    i1 = jnp.min(jnp.where(logits == m1, lane, big), axis=-1, keepdims=True)
    sel1 = lane == i1
    rest = jnp.where(sel1, neg, logits)
    m2 = jnp.max(rest, axis=-1, keepdims=True)
    i2 = jnp.min(jnp.where(rest == m2, lane, big), axis=-1, keepdims=True)
    sel2 = lane == i2
    e2 = jnp.exp(m2 - m1)
    g1 = 1.0 / (1.0 + e2)
    g2 = e2 / (1.0 + e2)
    onehot = jnp.where(sel1 | sel2, 1.0, 0.0)
    before = jnp.dot(tri_ref[...], onehot.astype(BF16), preferred_element_type=F32) + run_ref[...]
    r1 = jnp.sum(jnp.where(sel1, before, 0.0), axis=-1, keepdims=True)
    r2 = jnp.sum(jnp.where(sel2, before, 0.0), axis=-1, keepdims=True)
    rows = onehot.shape[0]
    total = before[rows - 1:rows, :] + onehot[rows - 1:rows, :]
    run_ref[...] = total
    cnt_ref[...] = jnp.broadcast_to(total, cnt_ref.shape)
    meta = jnp.where(lane == 0, g1, jnp.where(lane == 1, g2, 0.0))
    meta = jnp.where(lane == 2, i1.astype(F32), jnp.where(lane == 3, i2.astype(F32), meta))
    meta = jnp.where(lane == 4, r1, jnp.where(lane == 5, r2, meta))
    meta_ref[...] = meta


def _moe_plan(meta, counts_f, tm, ne):
    t = meta.shape[0]
    nt = TOP_K * t // tm + ne
    e_sel = meta[:, 2:2 + TOP_K].astype(jnp.int32)
    rank = meta[:, 2 + TOP_K:2 + 2 * TOP_K].astype(jnp.int32)
    counts = counts_f[0, :ne].astype(jnp.int32)
    padded = ((counts + tm - 1) // tm) * tm
    gend = jnp.cumsum(padded)
    gstart = gend - padded
    slot = gstart[e_sel] + rank
    tile_start = jnp.arange(nt, dtype=jnp.int32) * tm
    tile_e = jnp.sum((tile_start[:, None] >= gend[None, :]).astype(jnp.int32), axis=1)
    valid = (tile_e < ne).astype(jnp.int32)
    return slot, jnp.minimum(tile_e, ne - 1), valid, nt * tm


def _sc_geometry(t, d, dtype, row_buffers):
    sc = pltpu.get_tpu_info().sparse_core
    nc, nw = sc.num_cores, sc.num_cores * sc.num_subcores
    per_w = t // nw
    chunk = _tile(per_w, TOP_K * SC_GATHER_ROWS // row_buffers)
    assert t % nw == 0 and per_w % chunk == 0 and chunk % V7X_SUBLANES == 0
    assert row_buffers * chunk * d * jnp.dtype(dtype).itemsize + TOP_K * 4 * chunk < sc.vmem_capacity_bytes
    mesh = plsc.VectorSubcoreMesh(core_axis_name="c", subcore_axis_name="s")
    return nc, per_w, chunk, mesh


def _dispatch_rows(rows, slot0, slot1, n_slots):
    t, d = rows.shape
    nc, per_w, chunk, mesh = _sc_geometry(t, d, rows.dtype, 1)

    @functools.partial(
        pl.kernel, mesh=mesh, out_type=jax.ShapeDtypeStruct((n_slots, d), rows.dtype),
        scratch_types=[pltpu.VMEM((chunk,), jnp.int32), pltpu.VMEM((chunk,), jnp.int32),
                       pltpu.VMEM((chunk, d), rows.dtype)])
    def dispatch(rows_hbm, s0_hbm, s1_hbm, out_hbm, i0_v, i1_v, rows_v):
        base = (lax.axis_index("s") * nc + lax.axis_index("c")) * per_w

        @pl.loop(0, per_w // chunk)
        def _(c):
            off = pl.multiple_of(base + c * chunk, chunk)
            pltpu.sync_copy(s0_hbm.at[pl.ds(off, chunk)], i0_v)
            pltpu.sync_copy(s1_hbm.at[pl.ds(off, chunk)], i1_v)
            pltpu.sync_copy(rows_hbm.at[pl.ds(off, chunk)], rows_v)
            pltpu.sync_copy(rows_v, out_hbm.at[i0_v])
            pltpu.sync_copy(rows_v, out_hbm.at[i1_v])

    return dispatch(rows, slot0, slot1)


def _combine_rows(table, slot0, slot1):
    t = slot0.shape[0]
    d = table.shape[1]
    nc, per_w, chunk, mesh = _sc_geometry(t, d, table.dtype, TOP_K)
    out = jax.ShapeDtypeStruct((t, d), table.dtype)

    @functools.partial(
        pl.kernel, mesh=mesh, out_type=(out, out),
        scratch_types=[pltpu.VMEM((chunk,), jnp.int32), pltpu.VMEM((chunk,), jnp.int32),
                       pltpu.VMEM((chunk, d), table.dtype), pltpu.VMEM((chunk, d), table.dtype),
                       pltpu.SemaphoreType.DMA, pltpu.SemaphoreType.DMA])
    def combine(table_hbm, s0_hbm, s1_hbm, y0_hbm, y1_hbm, i0_v, i1_v, r0_v, r1_v, sem0, sem1):
        base = (lax.axis_index("s") * nc + lax.axis_index("c")) * per_w

        @pl.loop(0, per_w // chunk)
        def _(c):
            off = pl.multiple_of(base + c * chunk, chunk)
            pltpu.sync_copy(s0_hbm.at[pl.ds(off, chunk)], i0_v)
            pltpu.sync_copy(s1_hbm.at[pl.ds(off, chunk)], i1_v)
            g0 = pltpu.async_copy(table_hbm.at[i0_v], r0_v, sem0)
            g1 = pltpu.async_copy(table_hbm.at[i1_v], r1_v, sem1)
            g0.wait()
            pltpu.sync_copy(r0_v, y0_hbm.at[pl.ds(off, chunk)])
            g1.wait()
            pltpu.sync_copy(r1_v, y1_hbm.at[pl.ds(off, chunk)])

    return combine(table, slot0, slot1)


def _moe_kernel(te_ref, tv_ref, x_ref, g_ref, w1_ref, w3_ref, w2_ref, o_ref, n_ref, acc_ref):
    i = pl.program_id(0)
    f = pl.program_id(1)
    last = pl.num_programs(1) - 1
    valid = tv_ref[i] == 1

    @pl.when(valid & (f == 0))
    def _():
        n_ref[...] = _rms(x_ref[...], g_ref[...]).astype(BF16)
        acc_ref[...] = jnp.zeros_like(acc_ref)

    @pl.when(valid)
    def _():
        n = n_ref[...]
        w1 = w1_ref[0].astype(BF16)
        w3 = w3_ref[0].astype(BF16)
        w2 = w2_ref[0].astype(BF16)
        a = _silu(jnp.dot(n, w1, preferred_element_type=F32)) * jnp.dot(n, w3, preferred_element_type=F32)
        acc_ref[...] += jnp.dot(a.astype(BF16), w2, preferred_element_type=F32)

    @pl.when(valid & (f == last))
    def _():
        o_ref[...] = acc_ref[...]

    @pl.when(jnp.logical_not(valid) & (f == last))
    def _():
        o_ref[...] = jnp.zeros_like(o_ref)


def _moe_experts(xs, g, tile_e, tile_valid, w1, w3, w2, tm):
    ns, d = xs.shape
    ne, _, ff = w1.shape
    tf = _tile(ff, MOE_COL_TILE, V7X_LANES)
    nf = ff // tf
    fidx = lambda f, tv, i: f * tv[i] + (nf - 1) * (1 - tv[i])
    blocks = 2 * _nbytes((tm, d), F32) + 3 * _nbytes((d, tf), w1.dtype)
    scratch = _nbytes((tm, d), BF16) + _nbytes((tm, d), F32)
    temps = 4 * _nbytes((tm, tf), F32) + _nbytes((tm, d), F32) + 3 * _nbytes((d, tf), BF16)
    grid_spec = pltpu.PrefetchScalarGridSpec(
        num_scalar_prefetch=2,
        grid=(ns // tm, nf),
        in_specs=[pl.BlockSpec((tm, d), lambda i, f, te, tv: (i, 0)),
                  pl.BlockSpec((1, d), lambda i, f, te, tv: (0, 0)),
                  pl.BlockSpec((1, d, tf), lambda i, f, te, tv: (te[i], 0, fidx(f, tv, i))),
                  pl.BlockSpec((1, d, tf), lambda i, f, te, tv: (te[i], 0, fidx(f, tv, i))),
                  pl.BlockSpec((1, tf, d), lambda i, f, te, tv: (te[i], fidx(f, tv, i), 0))],
        out_specs=pl.BlockSpec((tm, d), lambda i, f, te, tv: (i, 0)),
        scratch_shapes=[pltpu.VMEM((tm, d), BF16), pltpu.VMEM((tm, d), F32)])
    return pl.pallas_call(
        _moe_kernel,
        grid_spec=grid_spec,
        out_shape=jax.ShapeDtypeStruct((ns, d), F32),
        compiler_params=_params(("arbitrary", "arbitrary"), _vmem_limit(blocks, scratch, temps)),
        name="moe_experts",
    )(tile_e, tile_valid, xs, g, w1, w3, w2)


def _combine_ple_kernel(h_ref, y0_ref, y1_ref, m_ref, p_ref, g_ref, wg_ref, wp_ref, fg_ref, o_ref):
    meta = m_ref[...]
    h = h_ref[...] + meta[:, 0:1] * y0_ref[...] + meta[:, 1:2] * y1_ref[...]
    n = _rms(h, g_ref[...]).astype(BF16)
    gate = _sigmoid(jnp.dot(n, wg_ref[...], preferred_element_type=F32))
    emb = jnp.dot(p_ref[...].astype(BF16), wp_ref[...], preferred_element_type=F32)
    o_ref[...] = _rms(h + emb * gate, fg_ref[...])


def _combine_ple(h, y0, y1, meta, p3, layer, g, w_gate, w_ple, final_g):
    t, d = h.shape
    pd = p3.shape[2]
    tm = _tile(t, ROW_TILE)
    row = lambda i: (i, 0)
    fix = lambda i: (0, 0)
    blocks = (4 * _nbytes((tm, d), F32) + _nbytes((tm, pd + V7X_LANES), F32) + _nbytes((d + pd, d), BF16))
    return pl.pallas_call(
        _combine_ple_kernel,
        grid=(t // tm,),
        in_specs=[pl.BlockSpec((tm, d), row), pl.BlockSpec((tm, d), row), pl.BlockSpec((tm, d), row),
                  pl.BlockSpec((tm, V7X_LANES), row),
                  pl.BlockSpec((None, tm, pd), lambda i: (layer, i, 0)), pl.BlockSpec((1, d), fix),
                  pl.BlockSpec((d, d), fix), pl.BlockSpec((pd, d), fix), pl.BlockSpec((1, d), fix)],
        out_specs=pl.BlockSpec((tm, d), row),
        out_shape=jax.ShapeDtypeStruct((t, d), F32),
        compiler_params=_params(("parallel",), _vmem_limit(blocks, 0, 4 * _nbytes((tm, d), F32))),
        name="combine_ple_final",
    )(h, y0, y1, meta, p3, g, w_gate, w_ple, final_g)


def kernel(x, p, mix_norm_g, ffn_norm_g, ple_norm_g, w_ple, w_ple_gate, ev_w_in, ev_conv_w, ev_conv_b, ev_dt_bias, ev_a_log, ev_d, ev_ssd_norm_g, ev_w_out, ev_ffn_w1, ev_ffn_w3, ev_ffn_w2, od_pw1_w, od_pw1_b, od_dw_w, od_dw_b, od_ln_g, od_ln_b, od_pw2_w, od_pw2_b, od_router_w, od_moe_w1, od_moe_w3, od_moe_w2, final_norm_g):
    b, s, d = x.shape
    t = b * s
    depth = p.shape[0]
    assert depth == 2 and ev_w_in.shape[0] == 1 and od_pw1_w.shape[0] == 1
    sbw = SB_HEADS * SB_HEAD_DIM
    ssdw = SSD_HEADS * SSD_HEAD_DIM
    xbcw = ssdw + 2 * SSD_GROUPS * SSD_STATE
    in_width = 3 * sbw + ssdw + xbcw + SSD_HEADS
    assert ev_w_in.shape[2] == in_width
    bf = lambda w: w.astype(BF16)
    vec = lambda v: v.reshape(1, -1)

    x2 = x.reshape(t, d)
    p3 = p.reshape(depth, t, -1)

    w_all = bf(ev_w_in[0])
    w_rest = w_all[:, 3 * sbw:]
    w_in = jnp.concatenate([w_all[:, :2 * sbw], w_rest,
                            jnp.zeros((d, V7X_LANES - SSD_HEADS), BF16)], axis=1)
    w_vt = w_all[:, 2 * sbw:3 * sbw].T
    qscale = LOG2E / math.sqrt(SB_HEAD_DIM)
    qk, vt, z, xbc, dt = _in_proj(x2, vec(mix_norm_g[0]), w_in, w_vt, sbw=sbw, zw=ssdw, xbcw=xbcw, qscale=qscale)
    attn = _sb_attention(qk, vt, b, s, heads=SB_HEADS, hd=SB_HEAD_DIM)
    ynorm = _ssd(xbc, z, dt, ev_conv_w[0], ev_conv_b[0], ev_dt_bias[0], ev_a_log[0], ev_d[0], ev_ssd_norm_g[0],
                 b, s, heads=SSD_HEADS, hd=SSD_HEAD_DIM, nstate=SSD_STATE)
    h = _out_proj(x2, attn, ynorm, bf(ev_w_out[0]))
    h = _ffn(h, vec(ffn_norm_g[0]), bf(ev_ffn_w1[0]), bf(ev_ffn_w3[0]), bf(ev_ffn_w2[0]))
    h, u = _ple_glu(h, p3, 0, vec(ple_norm_g[0]), bf(w_ple_gate[0]), bf(w_ple[0]),
                    vec(mix_norm_g[1]), bf(od_pw1_w[0]), vec(od_pw1_b[0]))
    ne = od_router_w.shape[2]
    wr = jnp.zeros((d, V7X_LANES), BF16).at[:, :ne].set(bf(od_router_w[0]))
    h, meta, counts = _dwconv_route(u, h, od_dw_w[0], od_dw_b[0], od_ln_g[0], od_ln_b[0], bf(od_pw2_w[0]),
                                       od_pw2_b[0], vec(ffn_norm_g[1]), wr, ne, b, s)
    tm = _tile(TOP_K * t, FFN_ROW_TILE)
    slot, tile_e, tile_valid, n_slots = _moe_plan(meta, counts, tm, ne)
    xs = _dispatch_rows(h, slot[:, 0], slot[:, 1], n_slots)
    ys = _moe_experts(xs, vec(ffn_norm_g[1]), tile_e, tile_valid, od_moe_w1[0], od_moe_w3[0], od_moe_w2[0], tm)
    y0, y1 = _combine_rows(ys, slot[:, 0], slot[:, 1])
    out = _combine_ple(h, y0, y1, meta, p3, 1, vec(ple_norm_g[1]), bf(w_ple_gate[1]), bf(w_ple[1]), vec(final_norm_g))
    return out.reshape(b, s, d)
```
